```python
import math
import jax, jax.numpy as jnp
from jax import lax
import numpy as np

D_MODEL = 2048
BATCH = 1
SEQ = 16384
DEPTH = 1
DEC_BATCH = 16
DEC_SEQ = 64
PAST_LEN = 4096

CHUNK = 64
EPS = 1e-6
D_CONV = D_MODEL
CONV_A_WIDTH = 31
SSD_EXPAND = 2
D_INNER = SSD_EXPAND * D_MODEL
SSD_HEAD_DIM = 64
SSD_HEADS = D_INNER // SSD_HEAD_DIM
SSD_GROUPS = 8
SSD_STATE = 128
D_XBC = D_INNER + 2 * SSD_GROUPS * SSD_STATE
CONV_B_WIDTH = 4
D_FF = 5632
CONV_F_WIDTH = 3
N_BRANCHES = 2
SPLIT_POINTS = (
    D_CONV,
    2 * D_CONV,
    2 * D_CONV + D_INNER,
    2 * D_CONV + D_INNER + D_XBC,
    2 * D_CONV + D_INNER + D_XBC + SSD_HEADS,
)
D_IN_PROJ = 2 * D_CONV + D_INNER + D_XBC + SSD_HEADS + N_BRANCHES * D_MODEL

kernel_name = "hybrid_conformer_ssd_streaming_step"


def rms_norm(x, g):
    x32 = x.astype(jnp.float32)
    y = x32 * lax.rsqrt(jnp.mean(x32 * x32, axis=-1, keepdims=True) + EPS)
    return (y * g.astype(jnp.float32)).astype(x.dtype)


def layer_norm(x, g, b):
    x32 = x.astype(jnp.float32)
    mu = jnp.mean(x32, axis=-1, keepdims=True)
    xc = x32 - mu
    var = jnp.mean(xc * xc, axis=-1, keepdims=True)
    y = xc * lax.rsqrt(var + EPS)
    return (y * g.astype(jnp.float32) + b.astype(jnp.float32)).astype(x.dtype)


def causal_dwconv(x, buf, w, b):
    width = w.shape[0]
    xp = jnp.concatenate([buf.astype(x.dtype), x], axis=1)
    y = lax.conv_general_dilated(
        xp, w[:, None, :].astype(x.dtype), window_strides=(1,), padding='VALID',
        dimension_numbers=('NWC', 'WIO', 'NWC'), feature_group_count=x.shape[-1])
    return y + b.astype(x.dtype), xp[:, xp.shape[1] - (width - 1):]


def ssd_scan(x, dt, a, bmat, cmat, state0, chunk):
    f32 = jnp.float32
    bsz, L, H, P = x.shape
    G, N = bmat.shape[2], bmat.shape[3]
    hg = H // G
    nc = L // chunk
    xc = x.astype(f32).reshape(bsz, nc, chunk, G, hg, P).transpose(1, 0, 2, 3, 4, 5)
    dtc = dt.astype(f32).reshape(bsz, nc, chunk, G, hg).transpose(1, 0, 2, 3, 4)
    bc = bmat.astype(f32).reshape(bsz, nc, chunk, G, N).transpose(1, 0, 2, 3, 4)
    cc = cmat.astype(f32).reshape(bsz, nc, chunk, G, N).transpose(1, 0, 2, 3, 4)
    af = a.astype(f32).reshape(G, hg)
    causal = jnp.tril(jnp.ones((chunk, chunk), dtype=bool))[None, :, :, None, None]

    def step(state, inp):
        xk, dtk, bk, ck = inp
        cum = jnp.cumsum(dtk * af, axis=1)
        seg = cum[:, :, None] - cum[:, None, :]
        decay = jnp.exp(jnp.where(causal, seg, -jnp.inf))
        scores = jnp.einsum('bign,bjgn->bijg', ck, bk)
        y = jnp.einsum('bijg,bijgh,bjgh,bjghp->bighp', scores, decay, dtk, xk)
        y = y + jnp.einsum('bign,bghpn,bigh->bighp', ck, state, jnp.exp(cum))
        last = cum[:, -1]
        wgt = dtk * jnp.exp(last[:, None] - cum)
        state = state * jnp.exp(last)[..., None, None] + jnp.einsum('bjgn,bjgh,bjghp->bghpn', bk, wgt, xk)
        return state, y

    state, ys = lax.scan(step, state0.astype(f32).reshape(bsz, G, hg, P, N), (xc, dtc, bc, cc))
    y = ys.transpose(1, 0, 2, 3, 4, 5).reshape(bsz, L, H, P)
    return y.astype(x.dtype), state.reshape(bsz, H, P, N).astype(state0.dtype)


def trunk_layer(x, buf_a, buf_b, ssd_state, buf_f,
                norm_mix_pre, w_in, b_gate, conv_a_w, conv_a_b, ln_a_g, ln_a_b, w_a_out,
                conv_b_w, conv_b_b, dt_bias, a_log, d_skip, ssd_norm_g, w_b_out, w_o, norm_mix_post,
                norm_ffn_pre, w_up, ffn_conv_w, ffn_conv_b, w_down, norm_ffn_post):
    bsz, L, _ = x.shape
    chunk = CHUNK if L % CHUNK == 0 else L
    h = rms_norm(x, norm_mix_pre)
    proj = h @ w_in
    a_val, a_gate, z, xbc, dt_raw, gate_logits = jnp.split(proj, SPLIT_POINTS, axis=-1)
    gates = jax.nn.sigmoid(gate_logits + b_gate)
    g_a, g_b = jnp.split(gates, N_BRANCHES, axis=-1)
    u = a_val * jax.nn.sigmoid(a_gate)
    u, new_buf_a = causal_dwconv(u, buf_a, conv_a_w, conv_a_b)
    u = jax.nn.silu(layer_norm(u, ln_a_g, ln_a_b))
    y_a = u @ w_a_out
    xbc, new_buf_b = causal_dwconv(xbc, buf_b, conv_b_w, conv_b_b)
    xbc = jax.nn.silu(xbc)
    xs, bm, cm = jnp.split(xbc, (D_INNER, D_INNER + SSD_GROUPS * SSD_STATE), axis=-1)
    xs = xs.reshape(bsz, L, SSD_HEADS, SSD_HEAD_DIM)
    bm = bm.reshape(bsz, L, SSD_GROUPS, SSD_STATE)
    cm = cm.reshape(bsz, L, SSD_GROUPS, SSD_STATE)
    dt = jax.nn.softplus((dt_raw + dt_bias).astype(jnp.float32))
    a = -jnp.exp(a_log.astype(jnp.float32))
    ys, new_state = ssd_scan(xs, dt, a, bm, cm, ssd_state, chunk)
    ys = (ys + d_skip[:, None] * xs).reshape(bsz, L, D_INNER)
    ys = rms_norm(ys * jax.nn.silu(z), ssd_norm_g)
    y_b = ys @ w_b_out
    merged = g_a * y_a + g_b * y_b
    x = x + rms_norm(merged @ w_o, norm_mix_post)
    h = rms_norm(x, norm_ffn_pre)
    up = h @ w_up
    up, new_buf_f = causal_dwconv(up, buf_f, ffn_conv_w, ffn_conv_b)
    u_gate, u_val = jnp.split(up, 2, axis=-1)
    x = x + rms_norm((jax.nn.gelu(u_gate) * u_val) @ w_down, norm_ffn_post)
    return x, new_buf_a, new_buf_b, new_state, new_buf_f


def setup_inputs(seed: int = 0) -> dict:
    key = jax.random.key(seed)
    ks = jax.random.split(key, 32)
    f32 = jnp.float32

    def nrm(k, shape, scale):
        return jax.random.normal(k, shape, f32) * scale

    def gain(k, n):
        return 1.0 + nrm(k, (DEPTH, n), 0.02)

    dt0 = jnp.exp(jax.random.uniform(ks[20], (DEPTH, SSD_HEADS), f32, math.log(1e-3), math.log(1e-1)))
    return {
        "x_prompt": nrm(ks[0], (BATCH, SEQ, D_MODEL), 1.0),
        "x_sample": nrm(ks[1], (DEC_BATCH, DEC_SEQ, D_MODEL), 1.0),
        "cache_conv_a": nrm(ks[2], (DEPTH, DEC_BATCH, CONV_A_WIDTH - 1, D_CONV), 0.5),
        "cache_conv_b": nrm(ks[3], (DEPTH, DEC_BATCH, CONV_B_WIDTH - 1, D_XBC), 1.0),
        "state_ssd": nrm(ks[4], (DEPTH, DEC_BATCH, SSD_HEADS, SSD_HEAD_DIM, SSD_STATE), 0.1),
        "cache_ffn_conv": nrm(ks[5], (DEPTH, DEC_BATCH, CONV_F_WIDTH - 1, 2 * D_FF), 1.0),
        "norm_mix_pre": gain(ks[6], D_MODEL),
        "w_in": nrm(ks[7], (DEPTH, D_MODEL, D_IN_PROJ), D_MODEL ** -0.5),
        "b_gate": nrm(ks[8], (DEPTH, N_BRANCHES * D_MODEL), 0.02),
        "conv_a_w": nrm(ks[9], (DEPTH, CONV_A_WIDTH, D_CONV), CONV_A_WIDTH ** -0.5),
        "conv_a_b": nrm(ks[10], (DEPTH, D_CONV), 0.02),
        "ln_a_g": gain(ks[11], D_CONV),
        "ln_a_b": nrm(ks[12], (DEPTH, D_CONV), 0.02),
        "w_a_out": nrm(ks[13], (DEPTH, D_CONV, D_MODEL), D_CONV ** -0.5),
        "conv_b_w": nrm(ks[14], (DEPTH, CONV_B_WIDTH, D_XBC), CONV_B_WIDTH ** -0.5),
        "conv_b_b": nrm(ks[15], (DEPTH, D_XBC), 0.02),
        "dt_bias": dt0 + jnp.log(-jnp.expm1(-dt0)),
        "a_log": jnp.log(jax.random.uniform(ks[16], (DEPTH, SSD_HEADS), f32, 1.0, 16.0)),
        "d_skip": gain(ks[17], SSD_HEADS),
        "ssd_norm_g": gain(ks[18], D_INNER),
        "w_b_out": nrm(ks[19], (DEPTH, D_INNER, D_MODEL), D_INNER ** -0.5),
        "w_o": nrm(ks[21], (DEPTH, D_MODEL, D_MODEL), D_MODEL ** -0.5),
        "norm_mix_post": gain(ks[22], D_MODEL),
        "norm_ffn_pre": gain(ks[23], D_MODEL),
        "w_up": nrm(ks[24], (DEPTH, D_MODEL, 2 * D_FF), D_MODEL ** -0.5),
        "ffn_conv_w": nrm(ks[25], (DEPTH, CONV_F_WIDTH, 2 * D_FF), CONV_F_WIDTH ** -0.5),
        "ffn_conv_b": nrm(ks[26], (DEPTH, 2 * D_FF), 0.02),
        "w_down": nrm(ks[27], (DEPTH, D_FF, D_MODEL), D_FF ** -0.5),
        "norm_ffn_post": gain(ks[28], D_MODEL),
    }


def reference(x_prompt, x_sample, cache_conv_a, cache_conv_b, state_ssd, cache_ffn_conv,
              norm_mix_pre, w_in, b_gate, conv_a_w, conv_a_b, ln_a_g, ln_a_b, w_a_out,
              conv_b_w, conv_b_b, dt_bias, a_log, d_skip, ssd_norm_g, w_b_out, w_o, norm_mix_post,
              norm_ffn_pre, w_up, ffn_conv_w, ffn_conv_b, w_down, norm_ffn_post):
    weights = (norm_mix_pre, w_in, b_gate, conv_a_w, conv_a_b, ln_a_g, ln_a_b, w_a_out,
               conv_b_w, conv_b_b, dt_bias, a_log, d_skip, ssd_norm_g, w_b_out, w_o, norm_mix_post,
               norm_ffn_pre, w_up, ffn_conv_w, ffn_conv_b, w_down, norm_ffn_post)
    bp = x_prompt.shape[0]
    dtype = x_prompt.dtype
    yp, ys = x_prompt, x_sample
    st_p = ([], [], [], [])
    st_s = ([], [], [], [])
    for layer in range(DEPTH):
        wl = [w[layer] for w in weights]
        zero_a = jnp.zeros((bp, CONV_A_WIDTH - 1, D_CONV), dtype)
        zero_b = jnp.zeros((bp, CONV_B_WIDTH - 1, D_XBC), dtype)
        zero_s = jnp.zeros((bp, SSD_HEADS, SSD_HEAD_DIM, SSD_STATE), dtype)
        zero_f = jnp.zeros((bp, CONV_F_WIDTH - 1, 2 * D_FF), dtype)
        yp, pa, pb, ps, pf = trunk_layer(yp, zero_a, zero_b, zero_s, zero_f, *wl)
        ys, sa, sb, ss, sf = trunk_layer(ys, cache_conv_a[layer], cache_conv_b[layer],
                                         state_ssd[layer], cache_ffn_conv[layer], *wl)
        for lst, v in zip(st_p, (pa, pb, ps, pf)):
            lst.append(v)
        for lst, v in zip(st_s, (sa, sb, ss, sf)):
            lst.append(v)
    new_conv_a_prompt = jnp.stack(st_p[0])
    new_conv_b_prompt = jnp.stack(st_p[1])
    new_ssd_prompt = jnp.stack(st_p[2])
    new_ffn_conv_prompt = jnp.stack(st_p[3])
    new_conv_a_sample = jnp.stack(st_s[0])
    new_conv_b_sample = jnp.stack(st_s[1])
    new_ssd_sample = jnp.stack(st_s[2])
    new_ffn_conv_sample = jnp.stack(st_s[3])
    return (yp, ys, new_conv_a_prompt, new_conv_b_prompt, new_ssd_prompt, new_ffn_conv_prompt,
            new_conv_a_sample, new_conv_b_sample, new_ssd_sample, new_ffn_conv_sample)
```

```python
import functools
import math

import jax
import jax.numpy as jnp
from jax import lax
from jax.experimental import pallas as pl
from jax.experimental.pallas import tpu as pltpu

F32 = jnp.float32
BF16 = jnp.bfloat16
EPS = 1e-6
LANES = 128
SUBLANES = 8
VMEM_LIMIT = 56 * 1024 * 1024
VMEM_BUDGET = 44 * 1024 * 1024
SSD_CHUNK = 128
NEG_BIG = -1e30


def _pick(n, cands):
    for c in cands:
        if n % c == 0:
            return c
    raise ValueError(f"no tile in {cands} divides {n}")


def _fit_rows(t, tm_max, *, k, bn, out_bn, x_bytes, out_bytes, has_norm, blk_bytes, w_buffers):
    tm = tm_max
    while True:
        per_row = 2 * k * x_bytes + 2 * out_bn * out_bytes + 2 * blk_bytes + 2 * bn * 4
        if has_norm:
            per_row += k * (2 + 4 + 4)
        if (tm * per_row + w_buffers * k * bn * 2 <= VMEM_BUDGET and t % tm == 0) or tm <= 64:
            return tm
        tm //= 2


def _cparams(sem):
    return pltpu.CompilerParams(dimension_semantics=sem, vmem_limit_bytes=VMEM_LIMIT)


def _sigmoid(x):
    return 1.0 / (1.0 + jnp.exp(-x))


def _silu(x):
    return x * _sigmoid(x)


def _softplus(x):
    return jnp.maximum(x, 0.0) + jnp.log1p(jnp.exp(-jnp.abs(x)))


def _rms_scale(x, eps):
    return x * lax.rsqrt(jnp.mean(x * x, axis=-1, keepdims=True) + eps)


def _mm_kernel(*refs, n_row, n_blk, has_norm, epilogue):
    x_ref, w_ref = refs[0], refs[1]
    pos = 2
    if has_norm:
        g_ref = refs[pos]
        pos += 1
    row_refs = refs[pos:pos + n_row]
    pos += n_row
    blk_refs = refs[pos:pos + n_blk]
    pos += n_blk
    o_ref = refs[pos]
    j = pl.program_id(1)
    if has_norm:
        h_ref = refs[pos + 1]

        @pl.when(j == 0)
        def _():
            x = x_ref[...].astype(F32)
            h_ref[...] = (_rms_scale(x, EPS) * g_ref[...]).astype(BF16)

        lhs = h_ref[...]
    else:
        lhs = x_ref[...]
    r = jnp.dot(lhs, w_ref[...], preferred_element_type=F32)
    epilogue(j, r, row_refs, blk_refs, o_ref)


def _matmul(x, w, *, tm, bn, out_cols, out_bn, out_dtype, epilogue, name,
            norm_gain=None, rows=(), blks=(), single_buffer_w=False):
    t, k = x.shape
    n = w.shape[1]
    tm = _fit_rows(t, tm, k=k, bn=bn, out_bn=out_bn, x_bytes=x.dtype.itemsize,
                   out_bytes=jnp.dtype(out_dtype).itemsize, has_norm=norm_gain is not None,
                   blk_bytes=sum(bw * a.dtype.itemsize for a, bw, _ in blks),
                   w_buffers=1 if single_buffer_w else 2)
    grid = (t // tm, n // bn)
    w_kwargs = {}
    if single_buffer_w:
        w_kwargs["pipeline_mode"] = pl.Buffered(1)
    in_specs = [pl.BlockSpec((tm, k), lambda i, j: (i, 0)),
                pl.BlockSpec((k, bn), lambda i, j: (0, j), **w_kwargs)]
    args = [x, w]
    if norm_gain is not None:
        in_specs.append(pl.BlockSpec((1, k), lambda i, j: (0, 0)))
        args.append(norm_gain)
    for arr, bw in rows:
        in_specs.append(pl.BlockSpec((1, bw), lambda i, j: (0, j)))
        args.append(arr)
    for arr, bw, off in blks:
        in_specs.append(pl.BlockSpec((tm, bw), lambda i, j, off=off: (i, j + off)))
        args.append(arr)
    scratch = [pltpu.VMEM((tm, k), BF16)] if norm_gain is not None else []
    return pl.pallas_call(
        functools.partial(_mm_kernel, n_row=len(rows), n_blk=len(blks),
                          has_norm=norm_gain is not None, epilogue=epilogue),
        grid=grid,
        in_specs=in_specs,
        out_specs=pl.BlockSpec((tm, out_bn), lambda i, j: (i, j)),
        out_shape=jax.ShapeDtypeStruct((t, out_cols), out_dtype),
        scratch_shapes=scratch,
        compiler_params=_cparams(("arbitrary", "arbitrary")),
        name=name,
    )(*args)


def _ep_glu(j, r, rows, blks, o_ref):
    half = r.shape[1] // 2
    o_ref[...] = (r[:, :half] * _sigmoid(r[:, half:])).astype(o_ref.dtype)


def _ep_silu(j, r, rows, blks, o_ref):
    o_ref[...] = _silu(r).astype(o_ref.dtype)


def _ep_store(j, r, rows, blks, o_ref):
    o_ref[...] = r.astype(o_ref.dtype)


def _ep_bias_sigmoid(j, r, rows, blks, o_ref):
    o_ref[...] = _sigmoid(r + rows[0][...]).astype(o_ref.dtype)


def _ep_bias_softplus(j, r, rows, blks, o_ref):
    o_ref[...] = _softplus(r + rows[0][...]).astype(o_ref.dtype)


def _ep_gate(j, r, rows, blks, o_ref):
    o_ref[...] = (r * blks[0][...].astype(F32)).astype(o_ref.dtype)


def _ep_gate_add(j, r, rows, blks, o_ref):
    o_ref[...] = (r * blks[0][...].astype(F32) + blks[1][...].astype(F32)).astype(o_ref.dtype)


def _ep_norm_residual(j, r, rows, blks, o_ref):
    o_ref[...] = (blks[0][...] + _rms_scale(r, EPS) * rows[0][...]).astype(o_ref.dtype)


def _conva_kernel(u_ref, hist_ref, w_ref, cb_ref, lg_ref, lb_ref, o_ref, bufp_ref, cv_ref,
                  *, width, hp, ta, rc, lc):
    c = u_ref.shape[1]
    t = pl.program_id(1)
    n_ext = hp + ta

    @pl.when(t == 0)
    def _():
        bufp_ref[0, 0:hp, :] = hist_ref[0]
        bufp_ref[0, n_ext:n_ext + SUBLANES, :] = jnp.zeros((SUBLANES, c), F32)

    bufp_ref[0, hp:hp + ta, :] = u_ref[...].astype(F32)

    def shift_body(r, carry):
        s0 = pl.multiple_of(r * rc, rc)
        win = bufp_ref[0, pl.ds(s0, rc + SUBLANES), :]
        for p in range(1, SUBLANES):
            bufp_ref[p, pl.ds(s0, rc), :] = win[p:p + rc, :]
        return carry

    lax.fori_loop(0, n_ext // rc, shift_body, 0)

    def conv_body(r, carry):
        r0 = pl.multiple_of(r * rc, rc)
        for l0 in range(0, c, lc):
            acc = jnp.zeros((rc, lc), F32)
            for k in range(width):
                off = hp - (width - 1) + k
                q8, p = (off // SUBLANES) * SUBLANES, off % SUBLANES
                xs = bufp_ref[p, pl.ds(r0 + q8, rc), l0:l0 + lc]
                acc = acc + xs * w_ref[k:k + 1, l0:l0 + lc]
            cv_ref[pl.ds(r0, rc), l0:l0 + lc] = acc
        return carry

    lax.fori_loop(0, ta // rc, conv_body, 0)
    bufp_ref[0, 0:hp, :] = bufp_ref[0, ta:ta + hp, :]

    v = cv_ref[...] + cb_ref[...]
    mu = jnp.mean(v, axis=-1, keepdims=True)
    vc = v - mu
    var = jnp.mean(vc * vc, axis=-1, keepdims=True)
    y = vc * lax.rsqrt(var + EPS) * lg_ref[...] + lb_ref[...]
    o_ref[...] = _silu(y).astype(o_ref.dtype)


def _conva_call(u, hist, w, cb, lg, lb, prev_out, *, row0, nseq, seqlen, name):
    t_all, c = u.shape
    width = w.shape[0]
    hp = hist.shape[1]
    ta = _pick(seqlen, (256, 128, 64))
    rc = 32
    lc = 256
    assert row0 % ta == 0 and (hp + ta) % rc == 0 and hp >= width - 1
    nt = seqlen // ta
    blk0 = row0 // ta
    row_map = lambda b, t: (blk0 + b * nt + t, 0)
    const = lambda b, t: (0, 0)
    in_specs = [pl.BlockSpec((ta, c), row_map),
                pl.BlockSpec((1, hp, c), lambda b, t: (b, 0, 0)),
                pl.BlockSpec((width, c), const),
                pl.BlockSpec((1, c), const),
                pl.BlockSpec((1, c), const),
                pl.BlockSpec((1, c), const)]
    args = [u, hist, w, cb, lg, lb]
    aliases = {}
    if prev_out is not None:
        in_specs.append(pl.BlockSpec(memory_space=pl.ANY))
        args.append(prev_out)
        aliases = {len(args) - 1: 0}

    def body(*refs):
        if prev_out is not None:
            refs = refs[:6] + refs[7:]
        _conva_kernel(*refs, width=width, hp=hp, ta=ta, rc=rc, lc=lc)

    return pl.pallas_call(
        body,
        grid=(nseq, nt),
        in_specs=in_specs,
        out_specs=pl.BlockSpec((ta, c), row_map),
        out_shape=jax.ShapeDtypeStruct((t_all, c), BF16),
        scratch_shapes=[pltpu.VMEM((SUBLANES, hp + ta + SUBLANES, c), F32), pltpu.VMEM((ta, c), F32)],
        input_output_aliases=aliases,
        compiler_params=_cparams(("arbitrary", "arbitrary")),
        name=name,
    )(*args)


def _ssd_kernel(xbc_ref, zs_ref, dt_ref, hist_ref, s0_ref, w_ref, cb_ref, a_ref, dsk_ref, ex_ref,
                y_ref, sout_ref,
                buf_ref, x3_ref, b3_ref, c3_ref, y3_ref, st_ref, cumt_ref, dtt_ref, wgt_ref,
                *, width, q, ngroups, hg, pdim, nstate):
    hp = SUBLANES
    gw = hg * pdim
    d_inner = ngroups * gw
    c = pl.program_id(1)
    nchunks = pl.num_programs(1)

    @pl.when(c == 0)
    def _():
        buf_ref[0:hp, :] = hist_ref[0]
        st_ref[...] = s0_ref[0]

    buf_ref[hp:hp + q, :] = xbc_ref[...].astype(F32)

    def conv_cols(l0, lw):
        acc = jnp.zeros((q, lw), F32) + cb_ref[:, l0:l0 + lw]
        for k in range(width):
            off = hp - (width - 1) + k
            acc = acc + buf_ref[off:off + q, l0:l0 + lw] * w_ref[k:k + 1, l0:l0 + lw]
        return _silu(acc)

    for g in range(ngroups):
        for l0 in range(0, gw, LANES):
            x3_ref[g, :, l0:l0 + LANES] = conv_cols(g * gw + l0, LANES)
        b3_ref[g] = conv_cols(d_inner + g * nstate, nstate)
        c3_ref[g] = conv_cols(d_inner + ngroups * nstate + g * nstate, nstate)
    buf_ref[0:hp, :] = buf_ref[q:q + hp, :]

    dt = dt_ref[...]
    da = dt * a_ref[...]
    ii = lax.broadcasted_iota(jnp.int32, (q, q), 0)
    jj = lax.broadcasted_iota(jnp.int32, (q, q), 1)
    causal = ii >= jj
    tri = jnp.where(causal, 1.0, 0.0).astype(BF16)
    da_hi = da.astype(BF16)
    da_lo = (da - da_hi.astype(F32)).astype(BF16)
    cum = (jnp.dot(tri, da_hi, preferred_element_type=F32)
           + jnp.dot(tri, da_lo, preferred_element_type=F32))
    cum_t = cum.T
    dt_t = dt.T
    last_t = cum_t[:, q - 1:q]
    nh = ngroups * hg
    cumt_ref[...] = cum_t[0:nh, :].reshape(ngroups, hg, q)
    dtt_ref[...] = dt_t[0:nh, :].reshape(ngroups, hg, q)
    wgt_ref[...] = (dt_t * jnp.exp(last_t - cum_t))[0:nh, :].reshape(ngroups, hg, q)

    e_last = jnp.exp(cum[q - SUBLANES:q, :])
    e_hi = e_last.astype(BF16)
    e_lo = (e_last - e_hi.astype(F32)).astype(BF16)
    dec = (jnp.dot(e_hi, ex_ref[...], preferred_element_type=F32)
           + jnp.dot(e_lo, ex_ref[...], preferred_element_type=F32))
    dec_row = dec[SUBLANES - 1:SUBLANES, :]

    lane = lax.broadcasted_iota(jnp.int32, (1, LANES), 1)
    lo_mask = lane < pdim

    for g in range(ngroups):
        bg = b3_ref[g]
        cg = c3_ref[g]
        scores = lax.dot_general(cg.astype(BF16), bg.astype(BF16), (((1,), (1,)), ((), ())),
                                 preferred_element_type=F32)
        bg_t = bg.T
        cum_g = cumt_ref[g]
        dt_g = dtt_ref[g]
        wg_g = wgt_ref[g]
        for hp2 in range(hg // 2):
            l0 = hp2 * LANES
            xpair = x3_ref[g, :, l0:l0 + LANES]
            spair = st_ref[g, :, l0:l0 + LANES]
            ypair = jnp.zeros((q, LANES), F32)
            upd = jnp.zeros((nstate, LANES), F32)
            for part in range(2):
                hh = 2 * hp2 + part
                mask = lo_mask if part == 0 else jnp.logical_not(lo_mask)
                x_h = jnp.where(mask, xpair, 0.0).astype(BF16)
                s_h = jnp.where(mask, spair, 0.0).astype(BF16)
                row = cum_g[hh:hh + 1, :]
                col = jnp.broadcast_to(row, (LANES, q)).T
                seg = col[:, 0:q] - row
                decay = jnp.exp(jnp.where(causal, seg, NEG_BIG))
                m_h = scores * decay * dt_g[hh:hh + 1, :]
                cs_h = cg * jnp.exp(col[:, 0:nstate])
                lhs = jnp.concatenate([m_h, cs_h], axis=1).astype(BF16)
                rhs = jnp.concatenate([x_h, s_h], axis=0)
                ypair = ypair + jnp.dot(lhs, rhs, preferred_element_type=F32)
                bw_t = (bg_t * wg_g[hh:hh + 1, :]).astype(BF16)
                upd = upd + jnp.dot(bw_t, x_h, preferred_element_type=F32)
            y3_ref[g, :, l0:l0 + LANES] = ypair
            c0 = g * gw + l0
            st_ref[g, :, l0:l0 + LANES] = spair * dec_row[:, c0:c0 + LANES] + upd

    for g in range(ngroups):
        xg = x3_ref[g]
        yg = y3_ref[g] + dsk_ref[:, g * gw:(g + 1) * gw] * xg
        y_ref[:, g * gw:(g + 1) * gw] = (yg * zs_ref[:, g * gw:(g + 1) * gw].astype(F32)).astype(y_ref.dtype)

    @pl.when(c == nchunks - 1)
    def _():
        sout_ref[0] = st_ref[...]


def _ssd_call(xbc, zs, dt, hist, s0, w, cb, a_row, dsk_row, ex, prev_out, *,
              row0, nseq, seqlen, ngroups, hg, pdim, nstate, name):
    t_all, dxbc = xbc.shape
    d_inner = zs.shape[1]
    width = w.shape[0]
    q = _pick(seqlen, (SSD_CHUNK, 64))
    assert row0 % q == 0 and pdim * 2 == LANES and hg % 2 == 0 and nstate == LANES
    nc = seqlen // q
    blk0 = row0 // q
    gw = hg * pdim
    row_map = lambda b, c: (blk0 + b * nc + c, 0)
    const = lambda b, c: (0, 0)
    seq3 = lambda b, c: (b, 0, 0)
    seq4 = lambda b, c: (b, 0, 0, 0)
    in_specs = [pl.BlockSpec((q, dxbc), row_map),
                pl.BlockSpec((q, d_inner), row_map),
                pl.BlockSpec((q, LANES), row_map),
                pl.BlockSpec((1, SUBLANES, dxbc), seq3),
                pl.BlockSpec((1, ngroups, nstate, gw), seq4),
                pl.BlockSpec((width, dxbc), const),
                pl.BlockSpec((1, dxbc), const),
                pl.BlockSpec((1, LANES), const),
                pl.BlockSpec((1, d_inner), const),
                pl.BlockSpec((LANES, d_inner), const)]
    args = [xbc, zs, dt, hist, s0, w, cb, a_row, dsk_row, ex]
    n_in = len(args)
    aliases = {}
    if prev_out is not None:
        in_specs.append(pl.BlockSpec(memory_space=pl.ANY))
        args.append(prev_out)
        aliases = {n_in: 0}

    def body(*refs):
        if prev_out is not None:
            refs = refs[:n_in] + refs[n_in + 1:]
        _ssd_kernel(*refs, width=width, q=q, ngroups=ngroups, hg=hg, pdim=pdim, nstate=nstate)

    return pl.pallas_call(
        body,
        grid=(nseq, nc),
        in_specs=in_specs,
        out_specs=[pl.BlockSpec((q, d_inner), row_map),
                   pl.BlockSpec((1, ngroups, nstate, gw), seq4)],
        out_shape=[jax.ShapeDtypeStruct((t_all, d_inner), BF16),
                   jax.ShapeDtypeStruct((nseq, ngroups, nstate, gw), F32)],
        scratch_shapes=[pltpu.VMEM((SUBLANES + q, dxbc), F32),
                        pltpu.VMEM((ngroups, q, gw), F32),
                        pltpu.VMEM((ngroups, q, nstate), F32),
                        pltpu.VMEM((ngroups, q, nstate), F32),
                        pltpu.VMEM((ngroups, q, gw), F32),
                        pltpu.VMEM((ngroups, nstate, gw), F32),
                        pltpu.VMEM((ngroups, hg, q), F32),
                        pltpu.VMEM((ngroups, hg, q), F32),
                        pltpu.VMEM((ngroups, hg, q), F32)],
        input_output_aliases=aliases,
        compiler_params=_cparams(("arbitrary", "arbitrary")),
        name=name,
    )(*args)


def _convf_kernel(up_ref, hist_ref, w_ref, cb_ref, o_ref, buf_ref, *, width, tf, lc):
    hp = SUBLANES
    dff = o_ref.shape[1]
    t = pl.program_id(1)

    @pl.when(t == 0)
    def _():
        buf_ref[0:hp, :] = hist_ref[0]

    buf_ref[hp:hp + tf, :] = up_ref[...].astype(F32)

    def conv_cols(l0):
        acc = jnp.zeros((tf, lc), F32) + cb_ref[:, l0:l0 + lc]
        for k in range(width):
            off = hp - (width - 1) + k
            acc = acc + buf_ref[off:off + tf, l0:l0 + lc] * w_ref[k:k + 1, l0:l0 + lc]
        return acc

    for l0 in range(0, dff, lc):
        gate = conv_cols(l0)
        val = conv_cols(dff + l0)
        o_ref[:, l0:l0 + lc] = (jax.nn.gelu(gate) * val).astype(o_ref.dtype)
    buf_ref[0:hp, :] = buf_ref[tf:tf + hp, :]


def _convf_call(up, hist, w, cb, prev_out, *, row0, nseq, seqlen, name):
    t_all, f2 = up.shape
    dff = f2 // 2
    width = w.shape[0]
    tf = _pick(seqlen, (128, 64))
    lc = _pick(dff, (512, 256, 128))
    assert row0 % tf == 0
    nt = seqlen // tf
    blk0 = row0 // tf
    row_map = lambda b, t: (blk0 + b * nt + t, 0)
    const = lambda b, t: (0, 0)
    in_specs = [pl.BlockSpec((tf, f2), row_map),
                pl.BlockSpec((1, SUBLANES, f2), lambda b, t: (b, 0, 0)),
                pl.BlockSpec((width, f2), const),
                pl.BlockSpec((1, f2), const)]
    args = [up, hist, w, cb]
    aliases = {}
    if prev_out is not None:
        in_specs.append(pl.BlockSpec(memory_space=pl.ANY))
        args.append(prev_out)
        aliases = {len(args) - 1: 0}

    def body(*refs):
        if prev_out is not None:
            refs = refs[:4] + refs[5:]
        _convf_kernel(*refs, width=width, tf=tf, lc=lc)

    return pl.pallas_call(
        body,
        grid=(nseq, nt),
        in_specs=in_specs,
        out_specs=pl.BlockSpec((tf, dff), row_map),
        out_shape=jax.ShapeDtypeStruct((t_all, dff), BF16),
        scratch_shapes=[pltpu.VMEM((SUBLANES + tf, f2), F32)],
        input_output_aliases=aliases,
        compiler_params=_cparams(("arbitrary", "arbitrary")),
        name=name,
    )(*args)


def _pad_hist(cache, hp):
    nseq, wm1, c = cache.shape
    return jnp.pad(cache.astype(F32), ((0, 0), (hp - wm1, 0), (0, 0)))


def _tail_rows(cache, seq, nseq, seqlen, keep):
    c = seq.shape[-1]
    take = min(keep, seqlen)
    tail = seq.reshape(nseq, seqlen, c)[:, seqlen - take:].astype(F32)
    if take < keep:
        tail = jnp.concatenate([cache[:, cache.shape[1] - (keep - take):].astype(F32), tail], axis=1)
    return tail


def _layer(x_all, segs, caches, wts):
    (norm_mix_pre, w_in, b_gate, conv_a_w, conv_a_b, ln_a_g, ln_a_b, w_a_out, conv_b_w, conv_b_b,
     dt_bias, a_log, d_skip, ssd_norm_g, w_b_out, w_o, norm_mix_post, norm_ffn_pre, w_up,
     ffn_conv_w, ffn_conv_b, w_down, norm_ffn_post) = wts
    t_all, d = x_all.shape
    dc = conv_a_w.shape[1]
    wa = conv_a_w.shape[0]
    dxbc = conv_b_w.shape[1]
    wb = conv_b_w.shape[0]
    nheads = a_log.shape[0]
    d_inner = ssd_norm_g.shape[0]
    pdim = d_inner // nheads
    nstate = caches[0][2].shape[-1]
    ngroups = (dxbc - d_inner) // (2 * nstate)
    hg = nheads // ngroups
    f2 = ffn_conv_w.shape[1]
    dff = f2 // 2
    wf = ffn_conv_w.shape[0]
    assert nheads <= LANES

    tm = _pick(t_all, (1024, 512, 256, 128, 64))
    row = lambda v: v.reshape(1, -1).astype(F32)

    o_val, o_gate, o_z, o_xbc, o_dt, o_g = 0, dc, 2 * dc, 2 * dc + d_inner, 2 * dc + d_inner + dxbc, \
        2 * dc + d_inner + dxbc + nheads
    bh = _pick(dc, (512, 256, 128))
    w_glu = jnp.concatenate(
        [w_in[:, o_val:o_gate].reshape(d, dc // bh, bh), w_in[:, o_gate:o_z].reshape(d, dc // bh, bh)],
        axis=2).reshape(d, 2 * dc).astype(BF16)
    g_pre = row(norm_mix_pre)
    u = _matmul(x_all, w_glu, tm=tm, bn=2 * bh, out_cols=dc, out_bn=bh, out_dtype=BF16,
                epilogue=_ep_glu, norm_gain=g_pre, name="inproj_glu")
    bz = _pick(d_inner, (1024, 512, 256, 128))
    zs = _matmul(x_all, w_in[:, o_z:o_xbc].astype(BF16), tm=tm, bn=bz, out_cols=d_inner, out_bn=bz,
                 out_dtype=BF16, epilogue=_ep_silu, norm_gain=g_pre, name="inproj_z")
    bx = _pick(dxbc, (1024, 512, 256, 128))
    xbc = _matmul(x_all, w_in[:, o_xbc:o_dt].astype(BF16), tm=tm, bn=bx, out_cols=dxbc, out_bn=bx,
                  out_dtype=BF16, epilogue=_ep_store, norm_gain=g_pre, name="inproj_xbc")
    bg = _pick(2 * d, (1024, 512, 256, 128))
    gates = _matmul(x_all, w_in[:, o_g:].astype(BF16), tm=tm, bn=bg, out_cols=2 * d, out_bn=bg,
                    out_dtype=BF16, epilogue=_ep_bias_sigmoid, norm_gain=g_pre,
                    rows=[(row(b_gate), bg)], name="inproj_gates")
    w_dt = jnp.pad(w_in[:, o_dt:o_g], ((0, 0), (0, LANES - nheads))).astype(BF16)
    dt_b = jnp.pad(dt_bias.astype(F32), (0, LANES - nheads)).reshape(1, LANES)
    dt = _matmul(x_all, w_dt, tm=tm, bn=LANES, out_cols=LANES, out_bn=LANES, out_dtype=F32,
                 epilogue=_ep_bias_softplus, norm_gain=g_pre, rows=[(dt_b, LANES)], name="inproj_dt")

    hp_a = -(-(wa - 1) // SUBLANES) * SUBLANES
    a_act = None
    for si, (row0, nseq, seqlen) in enumerate(segs):
        a_act = _conva_call(u, _pad_hist(caches[si][0], hp_a), conv_a_w.astype(F32), row(conv_a_b),
                            row(ln_a_g), row(ln_a_b), a_act, row0=row0, nseq=nseq, seqlen=seqlen,
                            name=f"conva_{si}")
    bd = _pick(d, (1024, 512, 256, 128))
    ya = _matmul(a_act, w_a_out.astype(BF16), tm=tm, bn=bd, out_cols=d, out_bn=bd, out_dtype=BF16,
                 epilogue=_ep_gate, blks=[(gates, bd, 0)], name="proj_a_out")

    a_row = jnp.pad(-jnp.exp(a_log.astype(F32)), (0, LANES - nheads)).reshape(1, LANES)
    dsk_row = jnp.repeat(d_skip.astype(F32), pdim).reshape(1, d_inner)
    ex = (jnp.arange(LANES)[:, None] == (jnp.arange(d_inner)[None, :] // pdim)).astype(BF16)
    yg = None
    new_states = []
    gw = hg * pdim
    for si, (row0, nseq, seqlen) in enumerate(segs):
        s0 = caches[si][2].astype(F32)
        s0 = s0.reshape(nseq, ngroups, gw, nstate).transpose(0, 1, 3, 2)
        yg, s_new = _ssd_call(xbc, zs, dt, _pad_hist(caches[si][1], SUBLANES), s0,
                              conv_b_w.astype(F32), row(conv_b_b), a_row, dsk_row, ex, yg,
                              row0=row0, nseq=nseq, seqlen=seqlen, ngroups=ngroups, hg=hg,
                              pdim=pdim, nstate=nstate, name=f"ssd_{si}")
        new_states.append(s_new.transpose(0, 1, 3, 2).reshape(nseq, nheads, pdim, nstate))
    merged = _matmul(yg, w_b_out.astype(BF16), tm=tm, bn=bd, out_cols=d, out_bn=bd, out_dtype=BF16,
                     epilogue=_ep_gate_add, norm_gain=row(ssd_norm_g),
                     blks=[(gates, bd, d // bd), (ya, bd, 0)], name="proj_b_out")
    tm2 = _pick(t_all, (512, 256, 128, 64))
    x1 = _matmul(merged, w_o.astype(BF16), tm=tm2, bn=d, out_cols=d, out_bn=d, out_dtype=F32,
                 epilogue=_ep_norm_residual, rows=[(row(norm_mix_post), d)], blks=[(x_all, d, 0)],
                 name="proj_o")

    bu = _pick(f2, (1024, 512, 256, 128))
    up = _matmul(x1, w_up.astype(BF16), tm=tm, bn=bu, out_cols=f2, out_bn=bu, out_dtype=BF16,
                 epilogue=_ep_store, norm_gain=row(norm_ffn_pre), name="ffn_up")
    act = None
    for si, (row0, nseq, seqlen) in enumerate(segs):
        act = _convf_call(up, _pad_hist(caches[si][3], SUBLANES), ffn_conv_w.astype(F32),
                          row(ffn_conv_b), act, row0=row0, nseq=nseq, seqlen=seqlen, name=f"convf_{si}")
    tm3 = _pick(t_all, (256, 128, 64))
    y = _matmul(act, w_down.astype(BF16), tm=tm3, bn=d, out_cols=d, out_bn=d, out_dtype=F32,
                epilogue=_ep_norm_residual, rows=[(row(norm_ffn_post), d)], blks=[(x1, d, 0)],
                single_buffer_w=True, name="ffn_down")

    new_caches = []
    for si, (row0, nseq, seqlen) in enumerate(segs):
        sl = slice(row0, row0 + nseq * seqlen)
        new_caches.append((_tail_rows(caches[si][0], u[sl], nseq, seqlen, wa - 1),
                           _tail_rows(caches[si][1], xbc[sl], nseq, seqlen, wb - 1),
                           new_states[si],
                           _tail_rows(caches[si][3], up[sl], nseq, seqlen, wf - 1)))
    return y, new_caches


def kernel(x_prompt, x_sample, cache_conv_a, cache_conv_b, state_ssd, cache_ffn_conv, norm_mix_pre, w_in, b_gate, conv_a_w, conv_a_b, ln_a_g, ln_a_b, w_a_out, conv_b_w, conv_b_b, dt_bias, a_log, d_skip, ssd_norm_g, w_b_out, w_o, norm_mix_post, norm_ffn_pre, w_up, ffn_conv_w, ffn_conv_b, w_down, norm_ffn_post):
    weights = (norm_mix_pre, w_in, b_gate, conv_a_w, conv_a_b, ln_a_g, ln_a_b, w_a_out, conv_b_w,
               conv_b_b, dt_bias, a_log, d_skip, ssd_norm_g, w_b_out, w_o, norm_mix_post,
               norm_ffn_pre, w_up, ffn_conv_w, ffn_conv_b, w_down, norm_ffn_post)
    depth = w_in.shape[0]
    bp, lp, d = x_prompt.shape
    bs, ls, _ = x_sample.shape
    tp = bp * lp
    segs = [(0, bp, lp), (tp, bs, ls)]
    x_all = jnp.concatenate([x_prompt.reshape(tp, d), x_sample.reshape(bs * ls, d)], axis=0)
    dt_in = x_prompt.dtype
    outs_p = ([], [], [], [])
    outs_s = ([], [], [], [])
    for layer in range(depth):
        wl = tuple(w[layer] for w in weights)
        zeros = (jnp.zeros((bp, conv_a_w.shape[1] - 1, conv_a_w.shape[2]), dt_in),
                 jnp.zeros((bp, conv_b_w.shape[1] - 1, conv_b_w.shape[2]), dt_in),
                 jnp.zeros((bp,) + state_ssd.shape[2:], dt_in),
                 jnp.zeros((bp, ffn_conv_w.shape[1] - 1, ffn_conv_w.shape[2]), dt_in))
        cache_s = (cache_conv_a[layer], cache_conv_b[layer], state_ssd[layer], cache_ffn_conv[layer])
        x_all, (new_p, new_s) = _layer(x_all, segs, [zeros, cache_s], wl)
        for lst, v in zip(outs_p, new_p):
            lst.append(v.astype(dt_in))
        for lst, v in zip(outs_s, new_s):
            lst.append(v.astype(dt_in))
    y_prompt = x_all[:tp].reshape(bp, lp, d)
    y_sample = x_all[tp:].reshape(bs, ls, d)
    return (y_prompt, y_sample,
            jnp.stack(outs_p[0]), jnp.stack(outs_p[1]), jnp.stack(outs_p[2]), jnp.stack(outs_p[3]),
            jnp.stack(outs_s[0]), jnp.stack(outs_s[1]), jnp.stack(outs_s[2]), jnp.stack(outs_s[3]))
```

```python
import functools

import jax
import jax.numpy as jnp
from jax import lax
from jax.experimental import pallas as pl
from jax.experimental.pallas import tpu as pltpu

F32 = jnp.float32
BF16 = jnp.bfloat16
EPS = 1e-6
LANES = 128
SUBLANES = 8
VMEM_LIMIT = 56 * 1024 * 1024
SSD_CHUNK = 128
NEG_BIG = -1e30


def _pick(n, cands):
    for c in cands:
        if n % c == 0:
            return c
    raise ValueError(f"no tile in {cands} divides {n}")


def _cparams(sem):
    return pltpu.CompilerParams(dimension_semantics=sem, vmem_limit_bytes=VMEM_LIMIT)


def _sigmoid(x):
    return 1.0 / (1.0 + jnp.exp(-x))


def _silu(x):
    return x * _sigmoid(x)


def _softplus(x):
    return jnp.maximum(x, 0.0) + jnp.log1p(jnp.exp(-jnp.abs(x)))


def _rms_scale(x, eps):
    return x * lax.rsqrt(jnp.mean(x * x, axis=-1, keepdims=True) + eps)


def _prenorm_kernel(xp_ref, xs_ref, g_ref, o_ref, *, n_p):
    i = pl.program_id(0)

    def run(x_ref):
        o_ref[...] = (_rms_scale(x_ref[...].astype(F32), EPS) * g_ref[...]).astype(o_ref.dtype)

    pl.when(i < n_p)(functools.partial(run, xp_ref))
    pl.when(i >= n_p)(functools.partial(run, xs_ref))


def _prenorm(xp, xs, gain, *, tm):
    tp, d = xp.shape
    ts = xs.shape[0]
    n_p = tp // tm
    return pl.pallas_call(
        functools.partial(_prenorm_kernel, n_p=n_p),
        grid=((tp + ts) // tm,),
        in_specs=[pl.BlockSpec((tm, d), lambda i: (jnp.minimum(i, n_p - 1), 0)),
                  pl.BlockSpec((tm, d), lambda i: (jnp.maximum(i - n_p, 0), 0)),
                  pl.BlockSpec((1, d), lambda i: (0, 0))],
        out_specs=pl.BlockSpec((tm, d), lambda i: (i, 0)),
        out_shape=jax.ShapeDtypeStruct((tp + ts, d), BF16),
        compiler_params=_cparams(("arbitrary",)),
        name="prenorm",
    )(xp, xs, gain)


def _mm_kernel(*refs, order, n_w, cast_w, n_rows, blk_dual, out_dual, n_p, epilogue):
    it = iter(refs)
    lhs_ref = next(it)
    w_refs = [next(it) for _ in range(n_w)]
    row_refs = [next(it) for _ in range(n_rows)]
    blk_refs = [tuple(next(it) for _ in range(2 if d else 1)) for d in blk_dual]
    out_refs = [tuple(next(it) for _ in range(2 if d else 1)) for d in out_dual]
    i = pl.program_id(1 if order == "col" else 0)
    if cast_w:
        wbuf = next(it)

        @pl.when(i == 0)
        def _():
            c0 = 0
            for wr in w_refs:
                wbuf[:, c0:c0 + wr.shape[1]] = wr[...].astype(BF16)
                c0 += wr.shape[1]

        w = wbuf[...]
    else:
        w = w_refs[0][...]
    r = jnp.dot(lhs_ref[...], w, preferred_element_type=F32)

    def run(sel):
        epilogue(r, row_refs, [b[sel if len(b) > 1 else 0] for b in blk_refs],
                 [o[sel if len(o) > 1 else 0] for o in out_refs])

    if any(blk_dual) or any(out_dual):
        pl.when(i < n_p)(functools.partial(run, 0))
        pl.when(i >= n_p)(functools.partial(run, 1))
    else:
        run(0)


def _matmul(lhs, ws, *, nj, order, tm, outs, epilogue, name, rows=(), blks=(), cast_w=False,
            n_p=None, single_buffer_w=False):
    t, k = lhs.shape
    ni = t // tm
    if order == "col":
        grid = (nj, ni)
        spec = lambda shape, fn, **kw: pl.BlockSpec(shape, lambda a, b: fn(b, a), **kw)
    else:
        grid = (ni, nj)
        spec = lambda shape, fn, **kw: pl.BlockSpec(shape, lambda a, b: fn(a, b), **kw)
    assert not cast_w or order == "col" or nj == 1
    prompt_row = lambda i: jnp.minimum(i, n_p - 1)
    decode_row = lambda i: jnp.maximum(i - n_p, 0)

    in_specs = [spec((tm, k), lambda i, j: (i, 0))]
    args = [lhs]
    w_kw = {"pipeline_mode": pl.Buffered(1)} if single_buffer_w else {}
    bn = 0
    for arr, bw, off in ws:
        in_specs.append(spec((k, bw), lambda i, j, off=off: (0, j + off), **w_kw))
        args.append(arr)
        bn += bw
    for arr, bw, off in rows:
        in_specs.append(spec((1, bw), lambda i, j, off=off: (0, j + off)))
        args.append(arr)
    for arr, bw, off in blks:
        if isinstance(arr, tuple):
            in_specs.append(spec((tm, bw), lambda i, j, off=off: (prompt_row(i), j + off)))
            in_specs.append(spec((tm, bw), lambda i, j, off=off: (decode_row(i), j + off)))
            args.extend(arr)
        else:
            in_specs.append(spec((tm, bw), lambda i, j, off=off: (i, j + off)))
            args.append(arr)
    out_specs, out_shape = [], []
    for cols, bw, dtype, dual in outs:
        if dual:
            out_specs.append(spec((tm, bw), lambda i, j: (prompt_row(i), j)))
            out_specs.append(spec((tm, bw), lambda i, j: (decode_row(i), j)))
            out_shape.append(jax.ShapeDtypeStruct((n_p * tm, cols), dtype))
            out_shape.append(jax.ShapeDtypeStruct(((ni - n_p) * tm, cols), dtype))
        else:
            out_specs.append(spec((tm, bw), lambda i, j: (i, j)))
            out_shape.append(jax.ShapeDtypeStruct((t, cols), dtype))
    scratch = [pltpu.VMEM((k, bn), BF16)] if cast_w else []
    res = pl.pallas_call(
        functools.partial(_mm_kernel, order=order, n_w=len(ws), cast_w=cast_w, n_rows=len(rows),
                          blk_dual=[isinstance(b[0], tuple) for b in blks],
                          out_dual=[o[3] for o in outs], n_p=n_p, epilogue=epilogue),
        grid=grid,
        in_specs=in_specs,
        out_specs=out_specs,
        out_shape=out_shape,
        scratch_shapes=scratch,
        compiler_params=_cparams(("arbitrary", "arbitrary")),
        name=name,
    )(*args)
    return res[0] if len(res) == 1 else res


def _ep_glu(r, rows, blks, outs):
    half = r.shape[1] // 2
    outs[0][...] = (r[:, :half] * _sigmoid(r[:, half:])).astype(outs[0].dtype)


def _ep_silu(r, rows, blks, outs):
    outs[0][...] = _silu(r).astype(outs[0].dtype)


def _ep_store(r, rows, blks, outs):
    outs[0][...] = r.astype(outs[0].dtype)


def _ep_bias_sigmoid(r, rows, blks, outs):
    outs[0][...] = _sigmoid(r + rows[0][...]).astype(outs[0].dtype)


def _ep_dt(r, rows, blks, outs, *, nheads):
    lane = lax.broadcasted_iota(jnp.int32, r.shape, 1)
    outs[0][...] = jnp.where(lane < nheads, _softplus(r + rows[0][...]), 0.0).astype(outs[0].dtype)


def _ep_gate(r, rows, blks, outs):
    outs[0][...] = (r * blks[0][...].astype(F32)).astype(outs[0].dtype)


def _ep_gate_add(r, rows, blks, outs):
    outs[0][...] = (r * blks[0][...].astype(F32) + blks[1][...].astype(F32)).astype(outs[0].dtype)


def _ep_residual_norm(r, rows, blks, outs):
    x1 = blks[0][...] + _rms_scale(r, EPS) * rows[0][...]
    outs[0][...] = x1
    outs[1][...] = (_rms_scale(x1, EPS) * rows[1][...]).astype(outs[1].dtype)


def _ep_residual(r, rows, blks, outs):
    outs[0][...] = blks[0][...] + _rms_scale(r, EPS) * rows[0][...]


def _conva_kernel(u_ref, hist_ref, w_ref, cb_ref, lg_ref, lb_ref, o_ref, bufp_ref, cv_ref,
                  *, width, hp, ta, rc, lc):
    c = u_ref.shape[1]
    t = pl.program_id(1)
    n_ext = hp + ta

    @pl.when(t == 0)
    def _():
        bufp_ref[0, 0:hp, :] = hist_ref[0]
        bufp_ref[0, n_ext:n_ext + SUBLANES, :] = jnp.zeros((SUBLANES, c), F32)

    bufp_ref[0, hp:hp + ta, :] = u_ref[...].astype(F32)

    def shift_body(r, carry):
        s0 = pl.multiple_of(r * rc, rc)
        win = bufp_ref[0, pl.ds(s0, rc + SUBLANES), :]
        for p in range(1, SUBLANES):
            bufp_ref[p, pl.ds(s0, rc), :] = win[p:p + rc, :]
        return carry

    lax.fori_loop(0, n_ext // rc, shift_body, 0)

    def conv_body(r, carry):
        r0 = pl.multiple_of(r * rc, rc)
        for l0 in range(0, c, lc):
            acc = jnp.zeros((rc, lc), F32)
            for k in range(width):
                off = hp - (width - 1) + k
                q8, p = (off // SUBLANES) * SUBLANES, off % SUBLANES
                xs = bufp_ref[p, pl.ds(r0 + q8, rc), l0:l0 + lc]
                acc = acc + xs * w_ref[k:k + 1, l0:l0 + lc]
            cv_ref[pl.ds(r0, rc), l0:l0 + lc] = acc
        return carry

    lax.fori_loop(0, ta // rc, conv_body, 0)
    bufp_ref[0, 0:hp, :] = bufp_ref[0, ta:ta + hp, :]

    v = cv_ref[...] + cb_ref[...]
    mu = jnp.mean(v, axis=-1, keepdims=True)
    vc = v - mu
    var = jnp.mean(vc * vc, axis=-1, keepdims=True)
    y = vc * lax.rsqrt(var + EPS) * lg_ref[...] + lb_ref[...]
    o_ref[...] = _silu(y).astype(o_ref.dtype)


def _conva_call(u, hist, w, cb, lg, lb, prev_out, *, row0, nseq, seqlen, name):
    t_all, c = u.shape
    width = w.shape[0]
    hp = hist.shape[1]
    ta = _pick(seqlen, (256, 128, 64))
    rc = 32
    lc = 256
    assert row0 % ta == 0 and (hp + ta) % rc == 0 and hp >= width - 1
    nt = seqlen // ta
    blk0 = row0 // ta
    row_map = lambda b, t: (blk0 + b * nt + t, 0)
    const = lambda b, t: (0, 0)
    in_specs = [pl.BlockSpec((ta, c), row_map),
                pl.BlockSpec((1, hp, c), lambda b, t: (b, 0, 0)),
                pl.BlockSpec((width, c), const),
                pl.BlockSpec((1, c), const),
                pl.BlockSpec((1, c), const),
                pl.BlockSpec((1, c), const)]
    args = [u, hist, w, cb, lg, lb]
    aliases = {}
    if prev_out is not None:
        in_specs.append(pl.BlockSpec(memory_space=pl.ANY))
        args.append(prev_out)
        aliases = {len(args) - 1: 0}

    def body(*refs):
        if prev_out is not None:
            refs = refs[:6] + refs[7:]
        _conva_kernel(*refs, width=width, hp=hp, ta=ta, rc=rc, lc=lc)

    return pl.pallas_call(
        body,
        grid=(nseq, nt),
        in_specs=in_specs,
        out_specs=pl.BlockSpec((ta, c), row_map),
        out_shape=jax.ShapeDtypeStruct((t_all, c), BF16),
        scratch_shapes=[pltpu.VMEM((SUBLANES, hp + ta + SUBLANES, c), F32), pltpu.VMEM((ta, c), F32)],
        input_output_aliases=aliases,
        compiler_params=_cparams(("arbitrary", "arbitrary")),
        name=name,
    )(*args)


def _ssd_kernel(xbc_ref, zs_ref, dt_ref, hist_ref, s0_ref, w_ref, cb_ref, a_ref, dsk_ref, ng_ref,
                ex_ref, y_ref, sout_ref,
                buf_ref, x3_ref, b3_ref, c3_ref, y3_ref, st_ref, cumt_ref, dtt_ref, wgt_ref,
                *, width, q, ngroups, hg, pdim, nstate):
    hp = SUBLANES
    gw = hg * pdim
    d_inner = ngroups * gw
    c = pl.program_id(1)
    nchunks = pl.num_programs(1)

    @pl.when(c == 0)
    def _():
        buf_ref[0:hp, :] = hist_ref[0]
        st_ref[...] = s0_ref[0]

    buf_ref[hp:hp + q, :] = xbc_ref[...].astype(F32)

    def conv_cols(l0, lw):
        acc = jnp.zeros((q, lw), F32) + cb_ref[:, l0:l0 + lw]
        for k in range(width):
            off = hp - (width - 1) + k
            acc = acc + buf_ref[off:off + q, l0:l0 + lw] * w_ref[k:k + 1, l0:l0 + lw]
        return _silu(acc)

    for g in range(ngroups):
        for l0 in range(0, gw, LANES):
            x3_ref[g, :, l0:l0 + LANES] = conv_cols(g * gw + l0, LANES)
        b3_ref[g] = conv_cols(d_inner + g * nstate, nstate)
        c3_ref[g] = conv_cols(d_inner + ngroups * nstate + g * nstate, nstate)
    buf_ref[0:hp, :] = buf_ref[q:q + hp, :]

    dt = dt_ref[...]
    da = dt * a_ref[...]
    ii = lax.broadcasted_iota(jnp.int32, (q, q), 0)
    jj = lax.broadcasted_iota(jnp.int32, (q, q), 1)
    causal = ii >= jj
    tri = jnp.where(causal, 1.0, 0.0).astype(BF16)
    da_hi = da.astype(BF16)
    da_lo = (da - da_hi.astype(F32)).astype(BF16)
    cum = (jnp.dot(tri, da_hi, preferred_element_type=F32)
           + jnp.dot(tri, da_lo, preferred_element_type=F32))
    cum_t = cum.T
    dt_t = dt.T
    last_t = cum_t[:, q - 1:q]
    nh = ngroups * hg
    cumt_ref[...] = cum_t[0:nh, :].reshape(ngroups, hg, q)
    dtt_ref[...] = dt_t[0:nh, :].reshape(ngroups, hg, q)
    wgt_ref[...] = (dt_t * jnp.exp(last_t - cum_t))[0:nh, :].reshape(ngroups, hg, q)

    e_last = jnp.exp(cum[q - SUBLANES:q, :])
    e_hi = e_last.astype(BF16)
    e_lo = (e_last - e_hi.astype(F32)).astype(BF16)
    dec = (jnp.dot(e_hi, ex_ref[...], preferred_element_type=F32)
           + jnp.dot(e_lo, ex_ref[...], preferred_element_type=F32))
    dec_row = dec[SUBLANES - 1:SUBLANES, :]

    lane = lax.broadcasted_iota(jnp.int32, (1, LANES), 1)
    lo_mask = lane < pdim

    for g in range(ngroups):
        bg = b3_ref[g]
        cg = c3_ref[g]
        scores = lax.dot_general(cg.astype(BF16), bg.astype(BF16), (((1,), (1,)), ((), ())),
                                 preferred_element_type=F32)
        bg_t = bg.T
        cum_g = cumt_ref[g]
        dt_g = dtt_ref[g]
        wg_g = wgt_ref[g]
        for hp2 in range(hg // 2):
            l0 = hp2 * LANES
            xpair = x3_ref[g, :, l0:l0 + LANES]
            spair = st_ref[g, :, l0:l0 + LANES]
            ypair = jnp.zeros((q, LANES), F32)
            upd = jnp.zeros((nstate, LANES), F32)
            for part in range(2):
                hh = 2 * hp2 + part
                mask = lo_mask if part == 0 else jnp.logical_not(lo_mask)
                x_h = jnp.where(mask, xpair, 0.0).astype(BF16)
                s_h = jnp.where(mask, spair, 0.0).astype(BF16)
                row = cum_g[hh:hh + 1, :]
                col = jnp.broadcast_to(row, (LANES, q)).T
                seg = col[:, 0:q] - row
                decay = jnp.exp(jnp.where(causal, seg, NEG_BIG))
                m_h = scores * decay * dt_g[hh:hh + 1, :]
                cs_h = cg * jnp.exp(col[:, 0:nstate])
                lhs = jnp.concatenate([m_h, cs_h], axis=1).astype(BF16)
                rhs = jnp.concatenate([x_h, s_h], axis=0)
                ypair = ypair + jnp.dot(lhs, rhs, preferred_element_type=F32)
                bw_t = (bg_t * wg_g[hh:hh + 1, :]).astype(BF16)
                upd = upd + jnp.dot(bw_t, x_h, preferred_element_type=F32)
            y3_ref[g, :, l0:l0 + LANES] = ypair
            c0 = g * gw + l0
            st_ref[g, :, l0:l0 + LANES] = spair * dec_row[:, c0:c0 + LANES] + upd

    ssq = jnp.zeros((q, 1), F32)
    for g in range(ngroups):
        cols = slice(g * gw, (g + 1) * gw)
        yg = (y3_ref[g] + dsk_ref[:, cols] * x3_ref[g]) * zs_ref[:, cols].astype(F32)
        y3_ref[g] = yg
        ssq = ssq + jnp.sum(yg * yg, axis=-1, keepdims=True)
    inv = lax.rsqrt(ssq / d_inner + EPS)
    for g in range(ngroups):
        cols = slice(g * gw, (g + 1) * gw)
        y_ref[:, cols] = (y3_ref[g] * inv * ng_ref[:, cols]).astype(y_ref.dtype)

    @pl.when(c == nchunks - 1)
    def _():
        sout_ref[0] = st_ref[...]


def _ssd_call(xbc, zs, dt, hist, s0, w, cb, a_row, dsk_row, ng_row, ex, prev_out, *,
              row0, nseq, seqlen, ngroups, hg, pdim, nstate, name):
    t_all, dxbc = xbc.shape
    d_inner = zs.shape[1]
    width = w.shape[0]
    q = _pick(seqlen, (SSD_CHUNK, 64))
    assert row0 % q == 0 and pdim * 2 == LANES and hg % 2 == 0 and nstate == LANES
    nc = seqlen // q
    blk0 = row0 // q
    gw = hg * pdim
    row_map = lambda b, c: (blk0 + b * nc + c, 0)
    const = lambda b, c: (0, 0)
    seq3 = lambda b, c: (b, 0, 0)
    seq4 = lambda b, c: (b, 0, 0, 0)
    in_specs = [pl.BlockSpec((q, dxbc), row_map),
                pl.BlockSpec((q, d_inner), row_map),
                pl.BlockSpec((q, LANES), row_map),
                pl.BlockSpec((1, SUBLANES, dxbc), seq3),
                pl.BlockSpec((1, ngroups, nstate, gw), seq4),
                pl.BlockSpec((width, dxbc), const),
                pl.BlockSpec((1, dxbc), const),
                pl.BlockSpec((1, LANES), const),
                pl.BlockSpec((1, d_inner), const),
                pl.BlockSpec((1, d_inner), const),
                pl.BlockSpec((LANES, d_inner), const)]
    args = [xbc, zs, dt, hist, s0, w, cb, a_row, dsk_row, ng_row, ex]
    n_in = len(args)
    aliases = {}
    if prev_out is not None:
        in_specs.append(pl.BlockSpec(memory_space=pl.ANY))
        args.append(prev_out)
        aliases = {n_in: 0}

    def body(*refs):
        if prev_out is not None:
            refs = refs[:n_in] + refs[n_in + 1:]
        _ssd_kernel(*refs, width=width, q=q, ngroups=ngroups, hg=hg, pdim=pdim, nstate=nstate)

    return pl.pallas_call(
        body,
        grid=(nseq, nc),
        in_specs=in_specs,
        out_specs=[pl.BlockSpec((q, d_inner), row_map),
                   pl.BlockSpec((1, ngroups, nstate, gw), seq4)],
        out_shape=[jax.ShapeDtypeStruct((t_all, d_inner), BF16),
                   jax.ShapeDtypeStruct((nseq, ngroups, nstate, gw), F32)],
        scratch_shapes=[pltpu.VMEM((SUBLANES + q, dxbc), F32),
                        pltpu.VMEM((ngroups, q, gw), F32),
                        pltpu.VMEM((ngroups, q, nstate), F32),
                        pltpu.VMEM((ngroups, q, nstate), F32),
                        pltpu.VMEM((ngroups, q, gw), F32),
                        pltpu.VMEM((ngroups, nstate, gw), F32),
                        pltpu.VMEM((ngroups, hg, q), F32),
                        pltpu.VMEM((ngroups, hg, q), F32),
                        pltpu.VMEM((ngroups, hg, q), F32)],
        input_output_aliases=aliases,
        compiler_params=_cparams(("arbitrary", "arbitrary")),
        name=name,
    )(*args)


def _convf_kernel(up_ref, hist_ref, w_ref, cb_ref, o_ref, buf_ref, *, width, tf, lc):
    hp = SUBLANES
    dff = o_ref.shape[1]
    t = pl.program_id(1)

    @pl.when(t == 0)
    def _():
        buf_ref[0:hp, :] = hist_ref[0]

    buf_ref[hp:hp + tf, :] = up_ref[...].astype(F32)

    def conv_cols(l0):
        acc = jnp.zeros((tf, lc), F32) + cb_ref[:, l0:l0 + lc]
        for k in range(width):
            off = hp - (width - 1) + k
            acc = acc + buf_ref[off:off + tf, l0:l0 + lc] * w_ref[k:k + 1, l0:l0 + lc]
        return acc

    for l0 in range(0, dff, lc):
        gate = conv_cols(l0)
        val = conv_cols(dff + l0)
        o_ref[:, l0:l0 + lc] = (jax.nn.gelu(gate) * val).astype(o_ref.dtype)
    buf_ref[0:hp, :] = buf_ref[tf:tf + hp, :]


def _convf_call(up, hist, w, cb, prev_out, *, row0, nseq, seqlen, name):
    t_all, f2 = up.shape
    dff = f2 // 2
    width = w.shape[0]
    tf = _pick(seqlen, (128, 64))
    lc = _pick(dff, (512, 256, 128))
    assert row0 % tf == 0
    nt = seqlen // tf
    blk0 = row0 // tf
    row_map = lambda b, t: (blk0 + b * nt + t, 0)
    const = lambda b, t: (0, 0)
    in_specs = [pl.BlockSpec((tf, f2), row_map),
                pl.BlockSpec((1, SUBLANES, f2), lambda b, t: (b, 0, 0)),
                pl.BlockSpec((width, f2), const),
                pl.BlockSpec((1, f2), const)]
    args = [up, hist, w, cb]
    aliases = {}
    if prev_out is not None:
        in_specs.append(pl.BlockSpec(memory_space=pl.ANY))
        args.append(prev_out)
        aliases = {len(args) - 1: 0}

    def body(*refs):
        if prev_out is not None:
            refs = refs[:4] + refs[5:]
        _convf_kernel(*refs, width=width, tf=tf, lc=lc)

    return pl.pallas_call(
        body,
        grid=(nseq, nt),
        in_specs=in_specs,
        out_specs=pl.BlockSpec((tf, dff), row_map),
        out_shape=jax.ShapeDtypeStruct((t_all, dff), BF16),
        scratch_shapes=[pltpu.VMEM((SUBLANES + tf, f2), F32)],
        input_output_aliases=aliases,
        compiler_params=_cparams(("arbitrary", "arbitrary")),
        name=name,
    )(*args)


def _pad_hist(cache, hp):
    nseq, wm1, c = cache.shape
    return jnp.pad(cache.astype(F32), ((0, 0), (hp - wm1, 0), (0, 0)))


def _tail_rows(cache, rows_all, row0, nseq, seqlen, keep):
    take = min(keep, seqlen)
    tail = jnp.stack([lax.slice_in_dim(rows_all, row0 + (b + 1) * seqlen - take, row0 + (b + 1) * seqlen)
                      for b in range(nseq)]).astype(F32)
    if take < keep:
        tail = jnp.concatenate([cache[:, cache.shape[1] - (keep - take):].astype(F32), tail], axis=1)
    return tail


def _layer(xp, xs, segs, caches, wts):
    (norm_mix_pre, w_in, b_gate, conv_a_w, conv_a_b, ln_a_g, ln_a_b, w_a_out, conv_b_w, conv_b_b,
     dt_bias, a_log, d_skip, ssd_norm_g, w_b_out, w_o, norm_mix_post, norm_ffn_pre, w_up,
     ffn_conv_w, ffn_conv_b, w_down, norm_ffn_post) = wts
    tp, d = xp.shape
    ts = xs.shape[0]
    t_all = tp + ts
    dc = conv_a_w.shape[1]
    wa = conv_a_w.shape[0]
    dxbc = conv_b_w.shape[1]
    wb = conv_b_w.shape[0]
    nheads = a_log.shape[0]
    d_inner = ssd_norm_g.shape[0]
    pdim = d_inner // nheads
    nstate = caches[0][2].shape[-1]
    ngroups = (dxbc - d_inner) // (2 * nstate)
    hg = nheads // ngroups
    f2 = ffn_conv_w.shape[1]
    wf = ffn_conv_w.shape[0]
    assert nheads <= LANES

    row = lambda v: v.reshape(1, -1).astype(F32)
    tm = _pick(ts, (1024, 512, 256, 128, 64))
    assert tp % tm == 0
    n_p = tp // tm

    h = _prenorm(xp, xs, row(norm_mix_pre), tm=tm)

    o_val, o_gate, o_z, o_xbc = 0, dc, 2 * dc, 2 * dc + d_inner
    o_dt = o_xbc + dxbc
    o_g = o_dt + nheads
    bh = _pick(dc, (512, 256, 128))
    u = _matmul(h, [(w_in, bh, o_val // bh), (w_in, bh, o_gate // bh)], nj=dc // bh, order="col", tm=tm,
                outs=[(dc, bh, BF16, False)], epilogue=_ep_glu, cast_w=True, name="inproj_glu")
    bz = _pick(d_inner, (1024, 512, 256, 128))
    assert o_z % bz == 0
    zs = _matmul(h, [(w_in, bz, o_z // bz)], nj=d_inner // bz, order="col", tm=tm,
                 outs=[(d_inner, bz, BF16, False)], epilogue=_ep_silu, cast_w=True, name="inproj_z")
    bx = _pick(dxbc, (1024, 512, 256, 128))
    assert o_xbc % bx == 0
    xbc = _matmul(h, [(w_in, bx, o_xbc // bx)], nj=dxbc // bx, order="col", tm=tm,
                  outs=[(dxbc, bx, BF16, False)], epilogue=_ep_store, cast_w=True, name="inproj_xbc")
    assert o_dt % LANES == 0 and o_dt + LANES <= w_in.shape[1]
    dt_b = jnp.pad(dt_bias.astype(F32), (0, LANES - nheads)).reshape(1, LANES)
    dt = _matmul(h, [(w_in, LANES, o_dt // LANES)], nj=1, order="col", tm=tm,
                 outs=[(LANES, LANES, F32, False)], epilogue=functools.partial(_ep_dt, nheads=nheads),
                 rows=[(dt_b, LANES, 0)], cast_w=True, name="inproj_dt")
    bg = _pick(2 * d, (1024, 512, 256, 128))
    gates = _matmul(h, [(w_in[:, o_g:].astype(BF16), bg, 0)], nj=2 * d // bg, order="col", tm=tm,
                    outs=[(2 * d, bg, BF16, False)], epilogue=_ep_bias_sigmoid,
                    rows=[(row(b_gate), bg, 0)], name="inproj_gates")

    hp_a = -(-(wa - 1) // SUBLANES) * SUBLANES
    a_act = None
    for si, (row0, nseq, seqlen) in enumerate(segs):
        a_act = _conva_call(u, _pad_hist(caches[si][0], hp_a), conv_a_w.astype(F32), row(conv_a_b),
                            row(ln_a_g), row(ln_a_b), a_act, row0=row0, nseq=nseq, seqlen=seqlen,
                            name=f"conva_{si}")
    bd = _pick(d, (1024, 512, 256, 128))
    ya = _matmul(a_act, [(w_a_out.astype(BF16), bd, 0)], nj=d // bd, order="row", tm=tm,
                 outs=[(d, bd, BF16, False)], epilogue=_ep_gate, blks=[(gates, bd, 0)], name="proj_a_out")

    a_row = jnp.pad(-jnp.exp(a_log.astype(F32)), (0, LANES - nheads)).reshape(1, LANES)
    dsk_row = jnp.repeat(d_skip.astype(F32), pdim).reshape(1, d_inner)
    ex = (jnp.arange(LANES)[:, None] == (jnp.arange(d_inner)[None, :] // pdim)).astype(BF16)
    yn = None
    new_states = []
    gw = hg * pdim
    for si, (row0, nseq, seqlen) in enumerate(segs):
        s0 = caches[si][2].astype(F32)
        s0 = s0.reshape(nseq, ngroups, gw, nstate).transpose(0, 1, 3, 2)
        yn, s_new = _ssd_call(xbc, zs, dt, _pad_hist(caches[si][1], SUBLANES), s0,
                              conv_b_w.astype(F32), row(conv_b_b), a_row, dsk_row, row(ssd_norm_g), ex, yn,
                              row0=row0, nseq=nseq, seqlen=seqlen, ngroups=ngroups, hg=hg,
                              pdim=pdim, nstate=nstate, name=f"ssd_{si}")
        new_states.append(s_new.transpose(0, 1, 3, 2).reshape(nseq, nheads, pdim, nstate))
    tm_b = _pick(tm, (512, 256, 128, 64))
    merged = _matmul(yn, [(w_b_out.astype(BF16), bd, 0)], nj=d // bd, order="row", tm=tm_b,
                     outs=[(d, bd, BF16, False)], epilogue=_ep_gate_add,
                     blks=[(gates, bd, d // bd), (ya, bd, 0)], name="proj_b_out")
    tm_o = _pick(tm, (256, 128, 64))
    x1, h2 = _matmul(merged, [(w_o.astype(BF16), d, 0)], nj=1, order="row", tm=tm_o,
                     outs=[(d, d, F32, False), (d, d, BF16, False)], epilogue=_ep_residual_norm,
                     rows=[(row(norm_mix_post), d, 0), (row(norm_ffn_pre), d, 0)],
                     blks=[((xp, xs), d, 0)], n_p=tp // tm_o, single_buffer_w=True, name="proj_o")

    bu = _pick(f2, (1024, 512, 256, 128))
    up = _matmul(h2, [(w_up, bu, 0)], nj=f2 // bu, order="col", tm=tm,
                 outs=[(f2, bu, BF16, False)], epilogue=_ep_store, cast_w=True, name="ffn_up")
    act = None
    for si, (row0, nseq, seqlen) in enumerate(segs):
        act = _convf_call(up, _pad_hist(caches[si][3], SUBLANES), ffn_conv_w.astype(F32),
                          row(ffn_conv_b), act, row0=row0, nseq=nseq, seqlen=seqlen, name=f"convf_{si}")
    yp, ys = _matmul(act, [(w_down.astype(BF16), d, 0)], nj=1, order="row", tm=tm_o,
                     outs=[(d, d, F32, True)], epilogue=_ep_residual,
                     rows=[(row(norm_ffn_post), d, 0)], blks=[(x1, d, 0)], n_p=tp // tm_o,
                     single_buffer_w=True, name="ffn_down")

    new_caches = []
    for si, (row0, nseq, seqlen) in enumerate(segs):
        new_caches.append((_tail_rows(caches[si][0], u, row0, nseq, seqlen, wa - 1),
                           _tail_rows(caches[si][1], xbc, row0, nseq, seqlen, wb - 1),
                           new_states[si],
                           _tail_rows(caches[si][3], up, row0, nseq, seqlen, wf - 1)))
    return yp, ys, new_caches


def kernel(x_prompt, x_sample, cache_conv_a, cache_conv_b, state_ssd, cache_ffn_conv, norm_mix_pre, w_in, b_gate, conv_a_w, conv_a_b, ln_a_g, ln_a_b, w_a_out, conv_b_w, conv_b_b, dt_bias, a_log, d_skip, ssd_norm_g, w_b_out, w_o, norm_mix_post, norm_ffn_pre, w_up, ffn_conv_w, ffn_conv_b, w_down, norm_ffn_post):
    weights = (norm_mix_pre, w_in, b_gate, conv_a_w, conv_a_b, ln_a_g, ln_a_b, w_a_out, conv_b_w,
               conv_b_b, dt_bias, a_log, d_skip, ssd_norm_g, w_b_out, w_o, norm_mix_post,
               norm_ffn_pre, w_up, ffn_conv_w, ffn_conv_b, w_down, norm_ffn_post)
    depth = w_in.shape[0]
    bp, lp, d = x_prompt.shape
    bs, ls, _ = x_sample.shape
    tp = bp * lp
    segs = [(0, bp, lp), (tp, bs, ls)]
    xp = x_prompt.reshape(tp, d)
    xs = x_sample.reshape(bs * ls, d)
    dt_in = x_prompt.dtype
    outs_p = ([], [], [], [])
    outs_s = ([], [], [], [])
    for layer in range(depth):
        wl = tuple(w[layer] for w in weights)
        zeros = (jnp.zeros((bp, conv_a_w.shape[1] - 1, conv_a_w.shape[2]), dt_in),
                 jnp.zeros((bp, conv_b_w.shape[1] - 1, conv_b_w.shape[2]), dt_in),
                 jnp.zeros((bp,) + state_ssd.shape[2:], dt_in),
                 jnp.zeros((bp, ffn_conv_w.shape[1] - 1, ffn_conv_w.shape[2]), dt_in))
        cache_s = (cache_conv_a[layer], cache_conv_b[layer], state_ssd[layer], cache_ffn_conv[layer])
        xp, xs, (new_p, new_s) = _layer(xp, xs, segs, [zeros, cache_s], wl)
        for lst, v in zip(outs_p, new_p):
            lst.append(v.astype(dt_in))
        for lst, v in zip(outs_s, new_s):
            lst.append(v.astype(dt_in))
    return (xp.reshape(bp, lp, d), xs.reshape(bs, ls, d),
            jnp.stack(outs_p[0]), jnp.stack(outs_p[1]), jnp.stack(outs_p[2]), jnp.stack(outs_p[3]),
            jnp.stack(outs_s[0]), jnp.stack(outs_s[1]), jnp.stack(outs_s[2]), jnp.stack(outs_s[3]))
```

```python
import functools

import jax
import jax.numpy as jnp
from jax import lax
from jax.experimental import pallas as pl
from jax.experimental.pallas import tpu as pltpu

F32 = jnp.float32
BF16 = jnp.bfloat16
EPS = 1e-6
LANES = 128
SUBLANES = 8
VMEM_LIMIT = 56 * 1024 * 1024
SSD_CHUNK = 128
NEG_BIG = -1e30


def _pick(n, cands):
    for c in cands:
        if n % c == 0:
            return c
    raise ValueError(f"no tile in {cands} divides {n}")


def _cparams(sem):
    return pltpu.CompilerParams(dimension_semantics=sem, vmem_limit_bytes=VMEM_LIMIT)


def _sigmoid(x):
    return 1.0 / (1.0 + jnp.exp(-x))


def _silu(x):
    return x * _sigmoid(x)


def _softplus(x):
    return jnp.maximum(x, 0.0) + jnp.log1p(jnp.exp(-jnp.abs(x)))


def _rms_scale(x, eps):
    return x * lax.rsqrt(jnp.mean(x * x, axis=-1, keepdims=True) + eps)


def _prenorm_kernel(xp_ref, xs_ref, g_ref, o_ref, *, n_p):
    i = pl.program_id(0)

    def run(x_ref):
        o_ref[...] = (_rms_scale(x_ref[...].astype(F32), EPS) * g_ref[...]).astype(o_ref.dtype)

    pl.when(i < n_p)(functools.partial(run, xp_ref))
    pl.when(i >= n_p)(functools.partial(run, xs_ref))


def _prenorm(xp, xs, gain, *, tm):
    tp, d = xp.shape
    ts = xs.shape[0]
    n_p = tp // tm
    return pl.pallas_call(
        functools.partial(_prenorm_kernel, n_p=n_p),
        grid=((tp + ts) // tm,),
        in_specs=[pl.BlockSpec((tm, d), lambda i: (jnp.minimum(i, n_p - 1), 0)),
                  pl.BlockSpec((tm, d), lambda i: (jnp.maximum(i - n_p, 0), 0)),
                  pl.BlockSpec((1, d), lambda i: (0, 0))],
        out_specs=pl.BlockSpec((tm, d), lambda i: (i, 0)),
        out_shape=jax.ShapeDtypeStruct((tp + ts, d), BF16),
        compiler_params=_cparams(("arbitrary",)),
        name="prenorm",
    )(xp, xs, gain)


def _mm_kernel(*refs, order, n_w, cast_w, w_t, n_rows, blk_dual, out_dual, n_p, epilogue):
    it = iter(refs)
    lhs_ref = next(it)
    w_refs = [next(it) for _ in range(n_w)]
    row_refs = [next(it) for _ in range(n_rows)]
    blk_refs = [tuple(next(it) for _ in range(2 if d else 1)) for d in blk_dual]
    out_refs = [tuple(next(it) for _ in range(2 if d else 1)) for d in out_dual]
    i = pl.program_id(1 if order == "col" else 0)
    if cast_w:
        wbuf = next(it)

        @pl.when(i == 0)
        def _():
            c0 = 0
            for wr in w_refs:
                if w_t:
                    wbuf[c0:c0 + wr.shape[0], :] = wr[...].astype(BF16)
                    c0 += wr.shape[0]
                else:
                    wbuf[:, c0:c0 + wr.shape[1]] = wr[...].astype(BF16)
                    c0 += wr.shape[1]

        w = wbuf[...]
    else:
        w = w_refs[0][...]
    dims = (((1,), (1,)), ((), ())) if w_t else (((1,), (0,)), ((), ()))
    r = lax.dot_general(lhs_ref[...], w, dims, preferred_element_type=F32)

    def run(sel):
        epilogue(r, row_refs, [b[sel if len(b) > 1 else 0] for b in blk_refs],
                 [o[sel if len(o) > 1 else 0] for o in out_refs])

    if any(blk_dual) or any(out_dual):
        pl.when(i < n_p)(functools.partial(run, 0))
        pl.when(i >= n_p)(functools.partial(run, 1))
    else:
        run(0)


def _matmul(lhs, ws, *, nj, order, tm, outs, epilogue, name, rows=(), blks=(), cast_w=False,
            w_t=False, n_p=None, single_buffer_w=False):
    t, k = lhs.shape
    ni = t // tm
    if order == "col":
        grid = (nj, ni)
        spec = lambda shape, fn, **kw: pl.BlockSpec(shape, lambda a, b: fn(b, a), **kw)
    else:
        grid = (ni, nj)
        spec = lambda shape, fn, **kw: pl.BlockSpec(shape, lambda a, b: fn(a, b), **kw)
    assert not cast_w or order == "col" or nj == 1
    prompt_row = lambda i: jnp.minimum(i, n_p - 1)
    decode_row = lambda i: jnp.maximum(i - n_p, 0)

    in_specs = [spec((tm, k), lambda i, j: (i, 0))]
    args = [lhs]
    w_kw = {"pipeline_mode": pl.Buffered(1)} if single_buffer_w else {}
    bn = 0
    for arr, bw, off in ws:
        if w_t:
            in_specs.append(spec((bw, k), lambda i, j, off=off: (j + off, 0), **w_kw))
        else:
            in_specs.append(spec((k, bw), lambda i, j, off=off: (0, j + off), **w_kw))
        args.append(arr)
        bn += bw
    for arr, bw, off in rows:
        in_specs.append(spec((1, bw), lambda i, j, off=off: (0, j + off)))
        args.append(arr)
    for arr, bw, off in blks:
        if isinstance(arr, tuple):
            in_specs.append(spec((tm, bw), lambda i, j, off=off: (prompt_row(i), j + off)))
            in_specs.append(spec((tm, bw), lambda i, j, off=off: (decode_row(i), j + off)))
            args.extend(arr)
        else:
            in_specs.append(spec((tm, bw), lambda i, j, off=off: (i, j + off)))
            args.append(arr)
    out_specs, out_shape = [], []
    for cols, bw, dtype, dual in outs:
        if dual:
            out_specs.append(spec((tm, bw), lambda i, j: (prompt_row(i), j)))
            out_specs.append(spec((tm, bw), lambda i, j: (decode_row(i), j)))
            out_shape.append(jax.ShapeDtypeStruct((n_p * tm, cols), dtype))
            out_shape.append(jax.ShapeDtypeStruct(((ni - n_p) * tm, cols), dtype))
        else:
            out_specs.append(spec((tm, bw), lambda i, j: (i, j)))
            out_shape.append(jax.ShapeDtypeStruct((t, cols), dtype))
    scratch = [pltpu.VMEM((bn, k) if w_t else (k, bn), BF16)] if cast_w else []
    res = pl.pallas_call(
        functools.partial(_mm_kernel, order=order, n_w=len(ws), cast_w=cast_w, w_t=w_t, n_rows=len(rows),
                          blk_dual=[isinstance(b[0], tuple) for b in blks],
                          out_dual=[o[3] for o in outs], n_p=n_p, epilogue=epilogue),
        grid=grid,
        in_specs=in_specs,
        out_specs=out_specs,
        out_shape=out_shape,
        scratch_shapes=scratch,
        compiler_params=_cparams(("arbitrary", "arbitrary")),
        name=name,
    )(*args)
    return res[0] if len(res) == 1 else res


def _ep_glu(r, rows, blks, outs):
    half = r.shape[1] // 2
    outs[0][...] = (r[:, :half] * _sigmoid(r[:, half:])).astype(outs[0].dtype)


def _ep_silu(r, rows, blks, outs):
    outs[0][...] = _silu(r).astype(outs[0].dtype)


def _ep_store(r, rows, blks, outs):
    outs[0][...] = r.astype(outs[0].dtype)


def _ep_bias_sigmoid(r, rows, blks, outs):
    outs[0][...] = _sigmoid(r + rows[0][...]).astype(outs[0].dtype)


def _ep_dt(r, rows, blks, outs, *, nheads):
    lane = lax.broadcasted_iota(jnp.int32, r.shape, 1)
    outs[0][...] = jnp.where(lane < nheads, _softplus(r + rows[0][...]), 0.0).astype(outs[0].dtype)


def _ep_gate(r, rows, blks, outs):
    outs[0][...] = (r * blks[0][...].astype(F32)).astype(outs[0].dtype)


def _ep_gate_add(r, rows, blks, outs):
    outs[0][...] = (r * blks[0][...].astype(F32) + blks[1][...].astype(F32)).astype(outs[0].dtype)


def _ep_residual_norm(r, rows, blks, outs):
    x1 = blks[0][...] + _rms_scale(r, EPS) * rows[0][...]
    outs[0][...] = x1
    outs[1][...] = (_rms_scale(x1, EPS) * rows[1][...]).astype(outs[1].dtype)


def _ep_residual(r, rows, blks, outs):
    outs[0][...] = blks[0][...] + _rms_scale(r, EPS) * rows[0][...]


def _conva_kernel(u_ref, hist_ref, w_ref, cb_ref, lg_ref, lb_ref, o_ref, bufp_ref, cv_ref,
                  *, width, hp, ta, rc, lc):
    c = u_ref.shape[1]
    t = pl.program_id(1)
    n_ext = hp + ta

    @pl.when(t == 0)
    def _():
        bufp_ref[0, 0:hp, :] = hist_ref[0]
        bufp_ref[0, n_ext:n_ext + SUBLANES, :] = jnp.zeros((SUBLANES, c), F32)

    bufp_ref[0, hp:hp + ta, :] = u_ref[...].astype(F32)

    def shift_body(r, carry):
        s0 = pl.multiple_of(r * rc, rc)
        win = bufp_ref[0, pl.ds(s0, rc + SUBLANES), :]
        for p in range(1, SUBLANES):
            bufp_ref[p, pl.ds(s0, rc), :] = win[p:p + rc, :]
        return carry

    lax.fori_loop(0, n_ext // rc, shift_body, 0)

    def conv_body(r, carry):
        r0 = pl.multiple_of(r * rc, rc)
        for l0 in range(0, c, lc):
            acc = jnp.zeros((rc, lc), F32)
            for k in range(width):
                off = hp - (width - 1) + k
                q8, p = (off // SUBLANES) * SUBLANES, off % SUBLANES
                xs = bufp_ref[p, pl.ds(r0 + q8, rc), l0:l0 + lc]
                acc = acc + xs * w_ref[k:k + 1, l0:l0 + lc]
            cv_ref[pl.ds(r0, rc), l0:l0 + lc] = acc
        return carry

    lax.fori_loop(0, ta // rc, conv_body, 0)
    bufp_ref[0, 0:hp, :] = bufp_ref[0, ta:ta + hp, :]

    v = cv_ref[...] + cb_ref[...]
    mu = jnp.mean(v, axis=-1, keepdims=True)
    vc = v - mu
    var = jnp.mean(vc * vc, axis=-1, keepdims=True)
    y = vc * lax.rsqrt(var + EPS) * lg_ref[...] + lb_ref[...]
    o_ref[...] = _silu(y).astype(o_ref.dtype)


def _conva_call(u, hist, w, cb, lg, lb, prev_out, *, row0, nseq, seqlen, name):
    t_all, c = u.shape
    width = w.shape[0]
    hp = hist.shape[1]
    ta = _pick(seqlen, (256, 128, 64))
    rc = 32
    lc = 256
    assert row0 % ta == 0 and (hp + ta) % rc == 0 and hp >= width - 1
    nt = seqlen // ta
    blk0 = row0 // ta
    row_map = lambda b, t: (blk0 + b * nt + t, 0)
    const = lambda b, t: (0, 0)
    in_specs = [pl.BlockSpec((ta, c), row_map),
                pl.BlockSpec((1, hp, c), lambda b, t: (b, 0, 0)),
                pl.BlockSpec((width, c), const),
                pl.BlockSpec((1, c), const),
                pl.BlockSpec((1, c), const),
                pl.BlockSpec((1, c), const)]
    args = [u, hist, w, cb, lg, lb]
    aliases = {}
    if prev_out is not None:
        in_specs.append(pl.BlockSpec(memory_space=pl.ANY))
        args.append(prev_out)
        aliases = {len(args) - 1: 0}

    def body(*refs):
        if prev_out is not None:
            refs = refs[:6] + refs[7:]
        _conva_kernel(*refs, width=width, hp=hp, ta=ta, rc=rc, lc=lc)

    return pl.pallas_call(
        body,
        grid=(nseq, nt),
        in_specs=in_specs,
        out_specs=pl.BlockSpec((ta, c), row_map),
        out_shape=jax.ShapeDtypeStruct((t_all, c), BF16),
        scratch_shapes=[pltpu.VMEM((SUBLANES, hp + ta + SUBLANES, c), F32), pltpu.VMEM((ta, c), F32)],
        input_output_aliases=aliases,
        compiler_params=_cparams(("arbitrary", "arbitrary")),
        name=name,
    )(*args)


def _ssd_kernel(xbc_ref, zs_ref, dt_ref, hist_ref, s0_ref, w_ref, cb_ref, a_ref, dsk_ref, ng_ref,
                ex_ref, y_ref, sout_ref,
                buf_ref, x3_ref, b3_ref, c3_ref, y3_ref, st_ref, cumt_ref, dtt_ref, wgt_ref,
                *, width, q, ngroups, hg, pdim, nstate):
    hp = SUBLANES
    gw = hg * pdim
    d_inner = ngroups * gw
    c = pl.program_id(1)
    nchunks = pl.num_programs(1)

    @pl.when(c == 0)
    def _():
        buf_ref[0:hp, :] = hist_ref[0]
        st_ref[...] = s0_ref[0]

    buf_ref[hp:hp + q, :] = xbc_ref[...].astype(F32)

    def conv_cols(l0, lw):
        acc = jnp.zeros((q, lw), F32) + cb_ref[:, l0:l0 + lw]
        for k in range(width):
            off = hp - (width - 1) + k
            acc = acc + buf_ref[off:off + q, l0:l0 + lw] * w_ref[k:k + 1, l0:l0 + lw]
        return _silu(acc)

    for g in range(ngroups):
        for l0 in range(0, gw, LANES):
            x3_ref[g, :, l0:l0 + LANES] = conv_cols(g * gw + l0, LANES)
        b3_ref[g] = conv_cols(d_inner + g * nstate, nstate)
        c3_ref[g] = conv_cols(d_inner + ngroups * nstate + g * nstate, nstate)
    buf_ref[0:hp, :] = buf_ref[q:q + hp, :]

    dt = dt_ref[...]
    da = dt * a_ref[...]
    ii = lax.broadcasted_iota(jnp.int32, (q, q), 0)
    jj = lax.broadcasted_iota(jnp.int32, (q, q), 1)
    causal = ii >= jj
    tri = jnp.where(causal, 1.0, 0.0).astype(BF16)
    da_hi = da.astype(BF16)
    da_lo = (da - da_hi.astype(F32)).astype(BF16)
    cum = (jnp.dot(tri, da_hi, preferred_element_type=F32)
           + jnp.dot(tri, da_lo, preferred_element_type=F32))
    cum_t = cum.T
    dt_t = dt.T
    last_t = cum_t[:, q - 1:q]
    nh = ngroups * hg
    cumt_ref[...] = cum_t[0:nh, :].reshape(ngroups, hg, q)
    dtt_ref[...] = dt_t[0:nh, :].reshape(ngroups, hg, q)
    wgt_ref[...] = (dt_t * jnp.exp(last_t - cum_t))[0:nh, :].reshape(ngroups, hg, q)

    e_last = jnp.exp(cum[q - SUBLANES:q, :])
    e_hi = e_last.astype(BF16)
    e_lo = (e_last - e_hi.astype(F32)).astype(BF16)
    dec = (jnp.dot(e_hi, ex_ref[...], preferred_element_type=F32)
           + jnp.dot(e_lo, ex_ref[...], preferred_element_type=F32))
    dec_row = dec[SUBLANES - 1:SUBLANES, :]

    lane = lax.broadcasted_iota(jnp.int32, (1, LANES), 1)
    lo_mask = lane < pdim

    for g in range(ngroups):
        bg = b3_ref[g]
        cg = c3_ref[g]
        scores = lax.dot_general(cg.astype(BF16), bg.astype(BF16), (((1,), (1,)), ((), ())),
                                 preferred_element_type=F32)
        bg_t = bg.T
        cum_g = cumt_ref[g]
        dt_g = dtt_ref[g]
        wg_g = wgt_ref[g]
        for hp2 in range(hg // 2):
            l0 = hp2 * LANES
            xpair = x3_ref[g, :, l0:l0 + LANES]
            spair = st_ref[g, :, l0:l0 + LANES]
            ypair = jnp.zeros((q, LANES), F32)
            upd = jnp.zeros((nstate, LANES), F32)
            for part in range(2):
                hh = 2 * hp2 + part
                mask = lo_mask if part == 0 else jnp.logical_not(lo_mask)
                x_h = jnp.where(mask, xpair, 0.0).astype(BF16)
                s_h = jnp.where(mask, spair, 0.0).astype(BF16)
                row = cum_g[hh:hh + 1, :]
                col = jnp.broadcast_to(row, (LANES, q)).T
                seg = col[:, 0:q] - row
                decay = jnp.exp(jnp.where(causal, seg, NEG_BIG))
                m_h = scores * decay * dt_g[hh:hh + 1, :]
                cs_h = cg * jnp.exp(col[:, 0:nstate])
                lhs = jnp.concatenate([m_h, cs_h], axis=1).astype(BF16)
                rhs = jnp.concatenate([x_h, s_h], axis=0)
                ypair = ypair + jnp.dot(lhs, rhs, preferred_element_type=F32)
                bw_t = (bg_t * wg_g[hh:hh + 1, :]).astype(BF16)
                upd = upd + jnp.dot(bw_t, x_h, preferred_element_type=F32)
            y3_ref[g, :, l0:l0 + LANES] = ypair
            c0 = g * gw + l0
            st_ref[g, :, l0:l0 + LANES] = spair * dec_row[:, c0:c0 + LANES] + upd

    ssq = jnp.zeros((q, 1), F32)
    for g in range(ngroups):
        cols = slice(g * gw, (g + 1) * gw)
        yg = (y3_ref[g] + dsk_ref[:, cols] * x3_ref[g]) * zs_ref[:, cols].astype(F32)
        y3_ref[g] = yg
        ssq = ssq + jnp.sum(yg * yg, axis=-1, keepdims=True)
    inv = lax.rsqrt(ssq / d_inner + EPS)
    for g in range(ngroups):
        cols = slice(g * gw, (g + 1) * gw)
        y_ref[:, cols] = (y3_ref[g] * inv * ng_ref[:, cols]).astype(y_ref.dtype)

    @pl.when(c == nchunks - 1)
    def _():
        sout_ref[0] = st_ref[...]


def _ssd_call(xbc, zs, dt, hist, s0, w, cb, a_row, dsk_row, ng_row, ex, prev_out, *,
              row0, nseq, seqlen, ngroups, hg, pdim, nstate, name):
    t_all, dxbc = xbc.shape
    d_inner = zs.shape[1]
    width = w.shape[0]
    q = _pick(seqlen, (SSD_CHUNK, 64))
    assert row0 % q == 0 and pdim * 2 == LANES and hg % 2 == 0 and nstate == LANES
    nc = seqlen // q
    blk0 = row0 // q
    gw = hg * pdim
    row_map = lambda b, c: (blk0 + b * nc + c, 0)
    const = lambda b, c: (0, 0)
    seq3 = lambda b, c: (b, 0, 0)
    seq4 = lambda b, c: (b, 0, 0, 0)
    in_specs = [pl.BlockSpec((q, dxbc), row_map),
                pl.BlockSpec((q, d_inner), row_map),
                pl.BlockSpec((q, LANES), row_map),
                pl.BlockSpec((1, SUBLANES, dxbc), seq3),
                pl.BlockSpec((1, ngroups, nstate, gw), seq4),
                pl.BlockSpec((width, dxbc), const),
                pl.BlockSpec((1, dxbc), const),
                pl.BlockSpec((1, LANES), const),
                pl.BlockSpec((1, d_inner), const),
                pl.BlockSpec((1, d_inner), const),
                pl.BlockSpec((LANES, d_inner), const)]
    args = [xbc, zs, dt, hist, s0, w, cb, a_row, dsk_row, ng_row, ex]
    n_in = len(args)
    aliases = {}
    if prev_out is not None:
        in_specs.append(pl.BlockSpec(memory_space=pl.ANY))
        args.append(prev_out)
        aliases = {n_in: 0}

    def body(*refs):
        if prev_out is not None:
            refs = refs[:n_in] + refs[n_in + 1:]
        _ssd_kernel(*refs, width=width, q=q, ngroups=ngroups, hg=hg, pdim=pdim, nstate=nstate)

    return pl.pallas_call(
        body,
        grid=(nseq, nc),
        in_specs=in_specs,
        out_specs=[pl.BlockSpec((q, d_inner), row_map),
                   pl.BlockSpec((1, ngroups, nstate, gw), seq4)],
        out_shape=[jax.ShapeDtypeStruct((t_all, d_inner), BF16),
                   jax.ShapeDtypeStruct((nseq, ngroups, nstate, gw), F32)],
        scratch_shapes=[pltpu.VMEM((SUBLANES + q, dxbc), F32),
                        pltpu.VMEM((ngroups, q, gw), F32),
                        pltpu.VMEM((ngroups, q, nstate), F32),
                        pltpu.VMEM((ngroups, q, nstate), F32),
                        pltpu.VMEM((ngroups, q, gw), F32),
                        pltpu.VMEM((ngroups, nstate, gw), F32),
                        pltpu.VMEM((ngroups, hg, q), F32),
                        pltpu.VMEM((ngroups, hg, q), F32),
                        pltpu.VMEM((ngroups, hg, q), F32)],
        input_output_aliases=aliases,
        compiler_params=_cparams(("arbitrary", "arbitrary")),
        name=name,
    )(*args)


def _ffnup_kernel(h_ref, wg_ref, wv_ref, cwg_ref, cwv_ref, cbg_ref, cbv_ref, hg_ref, hv_ref,
                  act_ref, tailp_ref, tails_ref, wbuf, cbuf, sbuf,
                  *, width, tm, bw, sm, n_p, tiles_per_seq, ls):
    hp = SUBLANES
    i = pl.program_id(1)

    @pl.when(i == 0)
    def _():
        wbuf[:, 0:bw] = wg_ref[...].astype(BF16)
        wbuf[:, bw:2 * bw] = wv_ref[...].astype(BF16)

    w = wbuf[...]

    def conv_act(buf, r0, n):
        def conv(c0, cw_ref, cb_ref):
            acc = jnp.zeros((n, bw), F32) + cb_ref[...]
            for k in range(width):
                off = r0 - (width - 1) + k
                acc = acc + buf[off:off + n, c0:c0 + bw] * cw_ref[k:k + 1, :]
            return acc
        return (jax.nn.gelu(conv(0, cwg_ref, cbg_ref)) * conv(bw, cwv_ref, cbv_ref)).astype(act_ref.dtype)

    @pl.when(i < n_p)
    def _():
        @pl.when(i % tiles_per_seq == 0)
        def _():
            cbuf[0:hp, :] = jnp.zeros((hp, 2 * bw), F32)

        for s in range(tm // sm):
            cbuf[hp + s * sm:hp + (s + 1) * sm, :] = jnp.dot(
                h_ref[s * sm:(s + 1) * sm, :], w, preferred_element_type=F32)
            act_ref[s * sm:(s + 1) * sm, :] = conv_act(cbuf, hp + s * sm, sm)

        @pl.when(i % tiles_per_seq == tiles_per_seq - 1)
        def _():
            tailp_ref[0, 0] = cbuf[tm:tm + hp, 0:bw]
            tailp_ref[0, 1] = cbuf[tm:tm + hp, bw:2 * bw]

        cbuf[0:hp, :] = cbuf[tm:tm + hp, :]

    @pl.when(i >= n_p)
    def _():
        for s in range(tm // sm):
            r = jnp.dot(h_ref[s * sm:(s + 1) * sm, :], w, preferred_element_type=F32)
            for q in range(sm // ls):
                seq = s * (sm // ls) + q
                sbuf[0:hp, 0:bw] = hg_ref[seq]
                sbuf[0:hp, bw:2 * bw] = hv_ref[seq]
                rq = r[q * ls:(q + 1) * ls, :]
                sbuf[hp:hp + ls, :] = rq
                act_ref[s * sm + q * ls:s * sm + (q + 1) * ls, :] = conv_act(sbuf, hp, ls)
                tails_ref[seq, 0] = rq[ls - hp:ls, 0:bw]
                tails_ref[seq, 1] = rq[ls - hp:ls, bw:2 * bw]


def _ffn_up_conv(h, w_up, conv_w, conv_b, hist_s, *, tm, n_p, tiles_per_seq, nseq_p, ls, name):
    t, k = h.shape
    f2 = w_up.shape[1]
    dff = f2 // 2
    width = conv_w.shape[0]
    bw = _pick(dff, (512, 256, 128))
    nj = dff // bw
    ni = t // tm
    sm = _pick(tm, (256, 128, 64))
    spt = tm // ls
    nseq_s = hist_s.shape[0]
    assert tm % ls == 0 and sm % ls == 0 and ls >= SUBLANES and width - 1 <= SUBLANES
    decode_tile = lambda i: jnp.maximum(i - n_p, 0)
    prompt_seq = lambda i: jnp.minimum(i, n_p - 1) // tiles_per_seq
    in_specs = [pl.BlockSpec((tm, k), lambda j, i: (i, 0)),
                pl.BlockSpec((k, bw), lambda j, i: (0, j)),
                pl.BlockSpec((k, bw), lambda j, i: (0, nj + j)),
                pl.BlockSpec((width, bw), lambda j, i: (0, j)),
                pl.BlockSpec((width, bw), lambda j, i: (0, nj + j)),
                pl.BlockSpec((1, bw), lambda j, i: (0, j)),
                pl.BlockSpec((1, bw), lambda j, i: (0, nj + j)),
                pl.BlockSpec((spt, SUBLANES, bw), lambda j, i: (decode_tile(i), 0, j)),
                pl.BlockSpec((spt, SUBLANES, bw), lambda j, i: (decode_tile(i), 0, nj + j))]
    out_specs = [pl.BlockSpec((tm, bw), lambda j, i: (i, j)),
                 pl.BlockSpec((1, 2, SUBLANES, bw), lambda j, i: (prompt_seq(i), 0, 0, j)),
                 pl.BlockSpec((spt, 2, SUBLANES, bw), lambda j, i: (decode_tile(i), 0, 0, j))]
    out_shape = [jax.ShapeDtypeStruct((t, dff), BF16),
                 jax.ShapeDtypeStruct((nseq_p, 2, SUBLANES, dff), F32),
                 jax.ShapeDtypeStruct((nseq_s, 2, SUBLANES, dff), F32)]
    return pl.pallas_call(
        functools.partial(_ffnup_kernel, width=width, tm=tm, bw=bw, sm=sm, n_p=n_p,
                          tiles_per_seq=tiles_per_seq, ls=ls),
        grid=(nj, ni),
        in_specs=in_specs,
        out_specs=out_specs,
        out_shape=out_shape,
        scratch_shapes=[pltpu.VMEM((k, 2 * bw), BF16),
                        pltpu.VMEM((SUBLANES + tm, 2 * bw), F32),
                        pltpu.VMEM((SUBLANES + ls, 2 * bw), F32)],
        compiler_params=_cparams(("arbitrary", "arbitrary")),
        name=name,
    )(h, w_up, w_up, conv_w, conv_w, conv_b, conv_b, hist_s, hist_s)


def _convf_kernel(up_ref, hist_ref, w_ref, cb_ref, o_ref, buf_ref, *, width, tf, lc):
    hp = SUBLANES
    dff = o_ref.shape[1]
    t = pl.program_id(1)

    @pl.when(t == 0)
    def _():
        buf_ref[0:hp, :] = hist_ref[0]

    buf_ref[hp:hp + tf, :] = up_ref[...].astype(F32)

    def conv_cols(l0):
        acc = jnp.zeros((tf, lc), F32) + cb_ref[:, l0:l0 + lc]
        for k in range(width):
            off = hp - (width - 1) + k
            acc = acc + buf_ref[off:off + tf, l0:l0 + lc] * w_ref[k:k + 1, l0:l0 + lc]
        return acc

    for l0 in range(0, dff, lc):
        gate = conv_cols(l0)
        val = conv_cols(dff + l0)
        o_ref[:, l0:l0 + lc] = (jax.nn.gelu(gate) * val).astype(o_ref.dtype)
    buf_ref[0:hp, :] = buf_ref[tf:tf + hp, :]


def _convf_call(up, hist, w, cb, prev_out, *, row0, nseq, seqlen, name):
    t_all, f2 = up.shape
    dff = f2 // 2
    width = w.shape[0]
    tf = _pick(seqlen, (128, 64))
    lc = _pick(dff, (512, 256, 128))
    assert row0 % tf == 0
    nt = seqlen // tf
    blk0 = row0 // tf
    row_map = lambda b, t: (blk0 + b * nt + t, 0)
    const = lambda b, t: (0, 0)
    in_specs = [pl.BlockSpec((tf, f2), row_map),
                pl.BlockSpec((1, SUBLANES, f2), lambda b, t: (b, 0, 0)),
                pl.BlockSpec((width, f2), const),
                pl.BlockSpec((1, f2), const)]
    args = [up, hist, w, cb]
    aliases = {}
    if prev_out is not None:
        in_specs.append(pl.BlockSpec(memory_space=pl.ANY))
        args.append(prev_out)
        aliases = {len(args) - 1: 0}

    def body(*refs):
        if prev_out is not None:
            refs = refs[:4] + refs[5:]
        _convf_kernel(*refs, width=width, tf=tf, lc=lc)

    return pl.pallas_call(
        body,
        grid=(nseq, nt),
        in_specs=in_specs,
        out_specs=pl.BlockSpec((tf, dff), row_map),
        out_shape=jax.ShapeDtypeStruct((t_all, dff), BF16),
        scratch_shapes=[pltpu.VMEM((SUBLANES + tf, f2), F32)],
        input_output_aliases=aliases,
        compiler_params=_cparams(("arbitrary", "arbitrary")),
        name=name,
    )(*args)


def _pad_hist(cache, hp):
    nseq, wm1, c = cache.shape
    return jnp.pad(cache.astype(F32), ((0, 0), (hp - wm1, 0), (0, 0)))


def _tail_rows(cache, rows_all, row0, nseq, seqlen, keep):
    take = min(keep, seqlen)
    tail = jnp.stack([lax.slice_in_dim(rows_all, row0 + (b + 1) * seqlen - take, row0 + (b + 1) * seqlen)
                      for b in range(nseq)]).astype(F32)
    if take < keep:
        tail = jnp.concatenate([cache[:, cache.shape[1] - (keep - take):].astype(F32), tail], axis=1)
    return tail


def _layer(xp, xs, segs, caches, wts):
    (norm_mix_pre, w_in, b_gate, conv_a_w, conv_a_b, ln_a_g, ln_a_b, w_a_out, conv_b_w, conv_b_b,
     dt_bias, a_log, d_skip, ssd_norm_g, w_b_out, w_o, norm_mix_post, norm_ffn_pre, w_up,
     ffn_conv_w, ffn_conv_b, w_down, norm_ffn_post) = wts
    tp, d = xp.shape
    ts = xs.shape[0]
    t_all = tp + ts
    dc = conv_a_w.shape[1]
    wa = conv_a_w.shape[0]
    dxbc = conv_b_w.shape[1]
    wb = conv_b_w.shape[0]
    nheads = a_log.shape[0]
    d_inner = ssd_norm_g.shape[0]
    pdim = d_inner // nheads
    nstate = caches[0][2].shape[-1]
    ngroups = (dxbc - d_inner) // (2 * nstate)
    hg = nheads // ngroups
    f2 = ffn_conv_w.shape[1]
    wf = ffn_conv_w.shape[0]
    assert nheads <= LANES

    row = lambda v: v.reshape(1, -1).astype(F32)
    tm = _pick(ts, (1024, 512, 256, 128, 64))
    assert tp % tm == 0
    n_p = tp // tm

    h = _prenorm(xp, xs, row(norm_mix_pre), tm=tm)

    w_in_t = jnp.swapaxes(w_in, 0, 1)
    o_val, o_gate, o_z, o_xbc = 0, dc, 2 * dc, 2 * dc + d_inner
    o_dt = o_xbc + dxbc
    o_g = o_dt + nheads
    bh = _pick(dc, (512, 256, 128))
    u = _matmul(h, [(w_in_t, bh, o_val // bh), (w_in_t, bh, o_gate // bh)], nj=dc // bh, order="col", tm=tm,
                outs=[(dc, bh, BF16, False)], epilogue=_ep_glu, cast_w=True, w_t=True, name="inproj_glu")
    bz = _pick(d_inner, (1024, 512, 256, 128))
    assert o_z % bz == 0
    zs = _matmul(h, [(w_in_t, bz, o_z // bz)], nj=d_inner // bz, order="col", tm=tm,
                 outs=[(d_inner, bz, BF16, False)], epilogue=_ep_silu, cast_w=True, w_t=True, name="inproj_z")
    bx = _pick(dxbc, (1024, 512, 256, 128))
    assert o_xbc % bx == 0
    xbc = _matmul(h, [(w_in_t, bx, o_xbc // bx)], nj=dxbc // bx, order="col", tm=tm,
                  outs=[(dxbc, bx, BF16, False)], epilogue=_ep_store, cast_w=True, w_t=True, name="inproj_xbc")
    assert o_dt % LANES == 0 and o_dt + LANES <= w_in_t.shape[0]
    dt_b = jnp.pad(dt_bias.astype(F32), (0, LANES - nheads)).reshape(1, LANES)
    dt = _matmul(h, [(w_in_t, LANES, o_dt // LANES)], nj=1, order="col", tm=tm,
                 outs=[(LANES, LANES, F32, False)], epilogue=functools.partial(_ep_dt, nheads=nheads),
                 rows=[(dt_b, LANES, 0)], cast_w=True, w_t=True, name="inproj_dt")
    bg = _pick(2 * d, (1024, 512, 256, 128))
    gates = _matmul(h, [(w_in_t[o_g:].astype(BF16), bg, 0)], nj=2 * d // bg, order="col", tm=tm,
                    outs=[(2 * d, bg, BF16, False)], epilogue=_ep_bias_sigmoid, w_t=True,
                    rows=[(row(b_gate), bg, 0)], name="inproj_gates")

    hp_a = -(-(wa - 1) // SUBLANES) * SUBLANES
    a_act = None
    for si, (row0, nseq, seqlen) in enumerate(segs):
        a_act = _conva_call(u, _pad_hist(caches[si][0], hp_a), conv_a_w.astype(F32), row(conv_a_b),
                            row(ln_a_g), row(ln_a_b), a_act, row0=row0, nseq=nseq, seqlen=seqlen,
                            name=f"conva_{si}")
    bd = _pick(d, (1024, 512, 256, 128))
    ya = _matmul(a_act, [(w_a_out.astype(BF16), bd, 0)], nj=d // bd, order="row", tm=tm,
                 outs=[(d, bd, BF16, False)], epilogue=_ep_gate, blks=[(gates, bd, 0)], name="proj_a_out")

    a_row = jnp.pad(-jnp.exp(a_log.astype(F32)), (0, LANES - nheads)).reshape(1, LANES)
    dsk_row = jnp.repeat(d_skip.astype(F32), pdim).reshape(1, d_inner)
    ex = (jnp.arange(LANES)[:, None] == (jnp.arange(d_inner)[None, :] // pdim)).astype(BF16)
    yn = None
    new_states = []
    gw = hg * pdim
    for si, (row0, nseq, seqlen) in enumerate(segs):
        s0 = caches[si][2].astype(F32)
        s0 = s0.reshape(nseq, ngroups, gw, nstate).transpose(0, 1, 3, 2)
        yn, s_new = _ssd_call(xbc, zs, dt, _pad_hist(caches[si][1], SUBLANES), s0,
                              conv_b_w.astype(F32), row(conv_b_b), a_row, dsk_row, row(ssd_norm_g), ex, yn,
                              row0=row0, nseq=nseq, seqlen=seqlen, ngroups=ngroups, hg=hg,
                              pdim=pdim, nstate=nstate, name=f"ssd_{si}")
        new_states.append(s_new.transpose(0, 1, 3, 2).reshape(nseq, nheads, pdim, nstate))
    tm_b = _pick(tm, (512, 256, 128, 64))
    merged = _matmul(yn, [(w_b_out.astype(BF16), bd, 0)], nj=d // bd, order="row", tm=tm_b,
                     outs=[(d, bd, BF16, False)], epilogue=_ep_gate_add,
                     blks=[(gates, bd, d // bd), (ya, bd, 0)], name="proj_b_out")
    tm_o = _pick(tm, (256, 128, 64))
    x1, h2 = _matmul(merged, [(w_o.astype(BF16), d, 0)], nj=1, order="row", tm=tm_o,
                     outs=[(d, d, F32, False), (d, d, BF16, False)], epilogue=_ep_residual_norm,
                     rows=[(row(norm_mix_post), d, 0), (row(norm_ffn_pre), d, 0)],
                     blks=[((xp, xs), d, 0)], n_p=tp // tm_o, single_buffer_w=True, name="proj_o")

    (_, nseq_p, len_p), (_, nseq_s, len_s) = segs
    assert len_p % tm == 0 and min(len_p, len_s) >= wf - 1
    act, tail_p, tail_s = _ffn_up_conv(h2, w_up, ffn_conv_w.astype(F32), row(ffn_conv_b),
                                       _pad_hist(caches[1][3], SUBLANES), tm=tm, n_p=n_p,
                                       tiles_per_seq=len_p // tm, nseq_p=nseq_p, ls=len_s, name="ffn_up")
    ffn_tails = [tl[:, :, SUBLANES - (wf - 1):].transpose(0, 2, 1, 3).reshape(tl.shape[0], wf - 1, f2)
                 for tl in (tail_p, tail_s)]
    yp, ys = _matmul(act, [(w_down.astype(BF16), d, 0)], nj=1, order="row", tm=tm_o,
                     outs=[(d, d, F32, True)], epilogue=_ep_residual,
                     rows=[(row(norm_ffn_post), d, 0)], blks=[(x1, d, 0)], n_p=tp // tm_o,
                     single_buffer_w=True, name="ffn_down")

    new_caches = []
    for si, (row0, nseq, seqlen) in enumerate(segs):
        new_caches.append((_tail_rows(caches[si][0], u, row0, nseq, seqlen, wa - 1),
                           _tail_rows(caches[si][1], xbc, row0, nseq, seqlen, wb - 1),
                           new_states[si],
                           ffn_tails[si]))
    return yp, ys, new_caches


def kernel(x_prompt, x_sample, cache_conv_a, cache_conv_b, state_ssd, cache_ffn_conv, norm_mix_pre, w_in, b_gate, conv_a_w, conv_a_b, ln_a_g, ln_a_b, w_a_out, conv_b_w, conv_b_b, dt_bias, a_log, d_skip, ssd_norm_g, w_b_out, w_o, norm_mix_post, norm_ffn_pre, w_up, ffn_conv_w, ffn_conv_b, w_down, norm_ffn_post):
    weights = (norm_mix_pre, w_in, b_gate, conv_a_w, conv_a_b, ln_a_g, ln_a_b, w_a_out, conv_b_w,
               conv_b_b, dt_bias, a_log, d_skip, ssd_norm_g, w_b_out, w_o, norm_mix_post,
               norm_ffn_pre, w_up, ffn_conv_w, ffn_conv_b, w_down, norm_ffn_post)
    depth = w_in.shape[0]
    bp, lp, d = x_prompt.shape
    bs, ls, _ = x_sample.shape
    tp = bp * lp
    segs = [(0, bp, lp), (tp, bs, ls)]
    xp = x_prompt.reshape(tp, d)
    xs = x_sample.reshape(bs * ls, d)
    dt_in = x_prompt.dtype
    outs_p = ([], [], [], [])
    outs_s = ([], [], [], [])
    for layer in range(depth):
        wl = tuple(w[layer] for w in weights)
        zeros = (jnp.zeros((bp, conv_a_w.shape[1] - 1, conv_a_w.shape[2]), dt_in),
                 jnp.zeros((bp, conv_b_w.shape[1] - 1, conv_b_w.shape[2]), dt_in),
                 jnp.zeros((bp,) + state_ssd.shape[2:], dt_in),
                 jnp.zeros((bp, ffn_conv_w.shape[1] - 1, ffn_conv_w.shape[2]), dt_in))
        cache_s = (cache_conv_a[layer], cache_conv_b[layer], state_ssd[layer], cache_ffn_conv[layer])
        xp, xs, (new_p, new_s) = _layer(xp, xs, segs, [zeros, cache_s], wl)
        for lst, v in zip(outs_p, new_p):
            lst.append(v.astype(dt_in))
        for lst, v in zip(outs_s, new_s):
            lst.append(v.astype(dt_in))
    return (xp.reshape(bp, lp, d), xs.reshape(bs, ls, d),
            jnp.stack(outs_p[0]), jnp.stack(outs_p[1]), jnp.stack(outs_p[2]), jnp.stack(outs_p[3]),
            jnp.stack(outs_s[0]), jnp.stack(outs_s[1]), jnp.stack(outs_s[2]), jnp.stack(outs_s[3]))
```

```python
import functools

import jax
import jax.numpy as jnp
from jax import lax
from jax.experimental import pallas as pl
from jax.experimental.pallas import tpu as pltpu

F32 = jnp.float32
BF16 = jnp.bfloat16
EPS = 1e-6
LANES = 128
SUBLANES = 8
VMEM_LIMIT = 56 * 1024 * 1024
SSD_CHUNK = 128
NEG_BIG = -1e30
NN_DIMS = (((1,), (0,)), ((), ()))
NT_DIMS = (((1,), (1,)), ((), ()))


def _pick(n, cands):
    for c in cands:
        if n % c == 0:
            return c
    raise ValueError(f"no tile in {cands} divides {n}")


def _cparams(sem):
    return pltpu.CompilerParams(dimension_semantics=sem, vmem_limit_bytes=VMEM_LIMIT)


def _sigmoid(x):
    return 1.0 / (1.0 + jnp.exp(-x))


def _silu(x):
    return x * _sigmoid(x)


def _softplus(x):
    return jnp.maximum(x, 0.0) + jnp.log1p(jnp.exp(-jnp.abs(x)))


def _rms_scale(x, eps):
    return x * lax.rsqrt(jnp.mean(x * x, axis=-1, keepdims=True) + eps)


def _prenorm_kernel(xp_ref, xs_ref, g_ref, o_ref, *, n_p):
    i = pl.program_id(0)

    def run(x_ref):
        o_ref[...] = (_rms_scale(x_ref[...].astype(F32), EPS) * g_ref[...]).astype(o_ref.dtype)

    pl.when(i < n_p)(functools.partial(run, xp_ref))
    pl.when(i >= n_p)(functools.partial(run, xs_ref))


def _prenorm(xp, xs, gain, *, tm):
    tp, d = xp.shape
    ts = xs.shape[0]
    n_p = tp // tm
    return pl.pallas_call(
        functools.partial(_prenorm_kernel, n_p=n_p),
        grid=((tp + ts) // tm,),
        in_specs=[pl.BlockSpec((tm, d), lambda i: (jnp.minimum(i, n_p - 1), 0)),
                  pl.BlockSpec((tm, d), lambda i: (jnp.maximum(i - n_p, 0), 0)),
                  pl.BlockSpec((1, d), lambda i: (0, 0))],
        out_specs=pl.BlockSpec((tm, d), lambda i: (i, 0)),
        out_shape=jax.ShapeDtypeStruct((tp + ts, d), BF16),
        compiler_params=_cparams(("arbitrary",)),
        name="prenorm",
    )(xp, xs, gain)


def _mm_kernel(*refs, order, n_w, cast_w, w_t, lhs_dual, n_rows, blk_dual, out_dual, n_p, epilogue):
    it = iter(refs)
    lhs_refs = tuple(next(it) for _ in range(2 if lhs_dual else 1))
    w_refs = [next(it) for _ in range(n_w)]
    row_refs = [next(it) for _ in range(n_rows)]
    blk_refs = [tuple(next(it) for _ in range(2 if d else 1)) for d in blk_dual]
    out_refs = [tuple(next(it) for _ in range(2 if d else 1)) for d in out_dual]
    i = pl.program_id(1 if order == "col" else 0)
    if cast_w:
        wbuf = next(it)

        @pl.when(i == 0)
        def _():
            c0 = 0
            for wr in w_refs:
                if w_t:
                    wbuf[c0:c0 + wr.shape[0], :] = wr[...].astype(BF16)
                    c0 += wr.shape[0]
                else:
                    wbuf[:, c0:c0 + wr.shape[1]] = wr[...].astype(BF16)
                    c0 += wr.shape[1]

        w_ref = wbuf
    else:
        w_ref = w_refs[0]

    def run(sel):
        pick = lambda pair: pair[sel if len(pair) > 1 else 0]
        r = lax.dot_general(pick(lhs_refs)[...], w_ref[...], NT_DIMS if w_t else NN_DIMS,
                            preferred_element_type=F32)
        epilogue(r, row_refs, [pick(b) for b in blk_refs], [pick(o) for o in out_refs])

    if lhs_dual or any(blk_dual) or any(out_dual):
        pl.when(i < n_p)(functools.partial(run, 0))
        pl.when(i >= n_p)(functools.partial(run, 1))
    else:
        run(0)


def _matmul(lhs, ws, *, nj, order, tm, outs, epilogue, name, rows=(), blks=(), cast_w=False,
            w_t=False, n_p=None, single_buffer_w=False):
    lhs_dual = isinstance(lhs, tuple)
    t = lhs[0].shape[0] + lhs[1].shape[0] if lhs_dual else lhs.shape[0]
    k = lhs[0].shape[1] if lhs_dual else lhs.shape[1]
    ni = t // tm
    if order == "col":
        grid = (nj, ni)
        spec = lambda shape, fn, **kw: pl.BlockSpec(shape, lambda a, b: fn(b, a), **kw)
    else:
        grid = (ni, nj)
        spec = lambda shape, fn, **kw: pl.BlockSpec(shape, lambda a, b: fn(a, b), **kw)
    assert not cast_w or order == "col" or nj == 1
    prompt_row = lambda i: jnp.minimum(i, n_p - 1)
    decode_row = lambda i: jnp.maximum(i - n_p, 0)

    if lhs_dual:
        in_specs = [spec((tm, k), lambda i, j: (prompt_row(i), 0)),
                    spec((tm, k), lambda i, j: (decode_row(i), 0))]
        args = list(lhs)
    else:
        in_specs = [spec((tm, k), lambda i, j: (i, 0))]
        args = [lhs]
    w_kw = {"pipeline_mode": pl.Buffered(1)} if single_buffer_w else {}
    bn = 0
    for arr, bw, off in ws:
        if w_t:
            in_specs.append(spec((bw, k), lambda i, j, off=off: (j + off, 0), **w_kw))
        else:
            in_specs.append(spec((k, bw), lambda i, j, off=off: (0, j + off), **w_kw))
        args.append(arr)
        bn += bw
    for arr, bw, off in rows:
        in_specs.append(spec((1, bw), lambda i, j, off=off: (0, j + off)))
        args.append(arr)
    for arr, bw, off in blks:
        if isinstance(arr, tuple):
            in_specs.append(spec((tm, bw), lambda i, j, off=off: (prompt_row(i), j + off)))
            in_specs.append(spec((tm, bw), lambda i, j, off=off: (decode_row(i), j + off)))
            args.extend(arr)
        else:
            in_specs.append(spec((tm, bw), lambda i, j, off=off: (i, j + off)))
            args.append(arr)
    out_specs, out_shape = [], []
    for cols, bw, dtype, dual in outs:
        if dual:
            out_specs.append(spec((tm, bw), lambda i, j: (prompt_row(i), j)))
            out_specs.append(spec((tm, bw), lambda i, j: (decode_row(i), j)))
            out_shape.append(jax.ShapeDtypeStruct((n_p * tm, cols), dtype))
            out_shape.append(jax.ShapeDtypeStruct(((ni - n_p) * tm, cols), dtype))
        else:
            out_specs.append(spec((tm, bw), lambda i, j: (i, j)))
            out_shape.append(jax.ShapeDtypeStruct((t, cols), dtype))
    scratch = [pltpu.VMEM((bn, k) if w_t else (k, bn), BF16)] if cast_w else []
    res = pl.pallas_call(
        functools.partial(_mm_kernel, order=order, n_w=len(ws), cast_w=cast_w, w_t=w_t,
                          lhs_dual=lhs_dual, n_rows=len(rows),
                          blk_dual=[isinstance(b[0], tuple) for b in blks],
                          out_dual=[o[3] for o in outs], n_p=n_p, epilogue=epilogue),
        grid=grid,
        in_specs=in_specs,
        out_specs=out_specs,
        out_shape=out_shape,
        scratch_shapes=scratch,
        compiler_params=_cparams(("arbitrary", "arbitrary")),
        name=name,
    )(*args)
    return res[0] if len(res) == 1 else res


def _ep_glu(r, rows, blks, outs):
    half = r.shape[1] // 2
    outs[0][...] = (r[:, :half] * _sigmoid(r[:, half:])).astype(outs[0].dtype)


def _ep_silu(r, rows, blks, outs):
    outs[0][...] = _silu(r).astype(outs[0].dtype)


def _ep_bias_sigmoid(r, rows, blks, outs):
    outs[0][...] = _sigmoid(r + rows[0][...]).astype(outs[0].dtype)


def _ep_dt(r, rows, blks, outs, *, nheads):
    lane = lax.broadcasted_iota(jnp.int32, r.shape, 1)
    outs[0][...] = jnp.where(lane < nheads, _softplus(r + rows[0][...]), 0.0).astype(outs[0].dtype)


def _ep_gate(r, rows, blks, outs):
    outs[0][...] = (r * blks[0][...].astype(F32)).astype(outs[0].dtype)


def _ep_gate_add(r, rows, blks, outs):
    outs[0][...] = (r * blks[0][...].astype(F32) + blks[1][...].astype(F32)).astype(outs[0].dtype)


def _ep_residual_norm(r, rows, blks, outs):
    x1 = blks[0][...] + _rms_scale(r, EPS) * rows[0][...]
    outs[0][...] = x1
    outs[1][...] = (_rms_scale(x1, EPS) * rows[1][...]).astype(outs[1].dtype)


def _ep_residual(r, rows, blks, outs):
    outs[0][...] = blks[0][...] + _rms_scale(r, EPS) * rows[0][...]


def _proj_conv_kernel(*refs, nw, w_t, width, tm, bw, sm, n_p, tiles_per_seq, ls, act_fn):
    it = iter(refs)
    h_ref = next(it)
    w_refs = [next(it) for _ in range(nw)]
    cw_refs = [next(it) for _ in range(nw)]
    cb_refs = [next(it) for _ in range(nw)]
    hist_refs = [next(it) for _ in range(nw)]
    act_ref, tailp_ref, tails_ref, wbuf, cbuf, sbuf = (next(it) for _ in range(6))
    hp = SUBLANES
    i = pl.program_id(1)

    @pl.when(i == 0)
    def _():
        for m, wr in enumerate(w_refs):
            if w_t:
                wbuf[m * bw:(m + 1) * bw, :] = wr[...].astype(BF16)
            else:
                wbuf[:, m * bw:(m + 1) * bw] = wr[...].astype(BF16)

    def project(r0):
        return lax.dot_general(h_ref[r0:r0 + sm, :], wbuf[...], NT_DIMS if w_t else NN_DIMS,
                               preferred_element_type=F32)

    def conv_act(buf, r0, n):
        convs = []
        for m in range(nw):
            acc = jnp.zeros((n, bw), F32) + cb_refs[m][...]
            for k in range(width):
                off = r0 - (width - 1) + k
                acc = acc + buf[off:off + n, m * bw:(m + 1) * bw] * cw_refs[m][k:k + 1, :]
            convs.append(acc)
        return act_fn(convs).astype(act_ref.dtype)

    @pl.when(i < n_p)
    def _():
        @pl.when(i % tiles_per_seq == 0)
        def _():
            cbuf[0:hp, :] = jnp.zeros((hp, nw * bw), F32)

        for s in range(tm // sm):
            cbuf[hp + s * sm:hp + (s + 1) * sm, :] = project(s * sm)
            act_ref[s * sm:(s + 1) * sm, :] = conv_act(cbuf, hp + s * sm, sm)

        @pl.when(i % tiles_per_seq == tiles_per_seq - 1)
        def _():
            for m in range(nw):
                tailp_ref[0, m] = cbuf[tm:tm + hp, m * bw:(m + 1) * bw]

        cbuf[0:hp, :] = cbuf[tm:tm + hp, :]

    @pl.when(i >= n_p)
    def _():
        for s in range(tm // sm):
            r = project(s * sm)
            for q in range(sm // ls):
                seq = s * (sm // ls) + q
                rq = r[q * ls:(q + 1) * ls, :]
                for m in range(nw):
                    sbuf[0:hp, m * bw:(m + 1) * bw] = hist_refs[m][seq]
                    tails_ref[seq, m] = rq[ls - hp:ls, m * bw:(m + 1) * bw]
                sbuf[hp:hp + ls, :] = rq
                act_ref[s * sm + q * ls:s * sm + (q + 1) * ls, :] = conv_act(sbuf, hp, ls)


def _proj_conv(h, w, w_offs, conv_w, conv_b, hist_s, c_offs, *, nj, bw, w_t, act_fn, tm, n_p,
               tiles_per_seq, nseq_p, ls, name):
    t, k = h.shape
    nw = len(w_offs)
    width = conv_w.shape[0]
    ni = t // tm
    sm = _pick(tm, (256, 128, 64))
    spt = tm // ls
    nseq_s = hist_s.shape[0]
    assert tm % ls == 0 and sm % ls == 0 and ls >= SUBLANES and width - 1 <= SUBLANES
    decode_tile = lambda i: jnp.maximum(i - n_p, 0)
    prompt_seq = lambda i: jnp.minimum(i, n_p - 1) // tiles_per_seq
    in_specs = [pl.BlockSpec((tm, k), lambda j, i: (i, 0))]
    args = [h]
    for off in w_offs:
        if w_t:
            in_specs.append(pl.BlockSpec((bw, k), lambda j, i, off=off: (j + off, 0)))
        else:
            in_specs.append(pl.BlockSpec((k, bw), lambda j, i, off=off: (0, j + off)))
        args.append(w)
    for off in c_offs:
        in_specs.append(pl.BlockSpec((width, bw), lambda j, i, off=off: (0, j + off)))
        args.append(conv_w)
    for off in c_offs:
        in_specs.append(pl.BlockSpec((1, bw), lambda j, i, off=off: (0, j + off)))
        args.append(conv_b)
    for off in c_offs:
        in_specs.append(pl.BlockSpec((spt, SUBLANES, bw), lambda j, i, off=off: (decode_tile(i), 0, j + off)))
        args.append(hist_s)
    out_specs = [pl.BlockSpec((tm, bw), lambda j, i: (i, j)),
                 pl.BlockSpec((1, nw, SUBLANES, bw), lambda j, i: (prompt_seq(i), 0, 0, j)),
                 pl.BlockSpec((spt, nw, SUBLANES, bw), lambda j, i: (decode_tile(i), 0, 0, j))]
    out_shape = [jax.ShapeDtypeStruct((t, nj * bw), BF16),
                 jax.ShapeDtypeStruct((nseq_p, nw, SUBLANES, nj * bw), F32),
                 jax.ShapeDtypeStruct((nseq_s, nw, SUBLANES, nj * bw), F32)]
    return pl.pallas_call(
        functools.partial(_proj_conv_kernel, nw=nw, w_t=w_t, width=width, tm=tm, bw=bw, sm=sm, n_p=n_p,
                          tiles_per_seq=tiles_per_seq, ls=ls, act_fn=act_fn),
        grid=(nj, ni),
        in_specs=in_specs,
        out_specs=out_specs,
        out_shape=out_shape,
        scratch_shapes=[pltpu.VMEM((nw * bw, k) if w_t else (k, nw * bw), BF16),
                        pltpu.VMEM((SUBLANES + tm, nw * bw), F32),
                        pltpu.VMEM((SUBLANES + ls, nw * bw), F32)],
        compiler_params=_cparams(("arbitrary", "arbitrary")),
        name=name,
    )(*args)


def _act_silu(convs):
    return _silu(convs[0])


def _act_gelu_gate(convs):
    return jax.nn.gelu(convs[0]) * convs[1]


def _conva_kernel(u_ref, hist_ref, w_ref, cb_ref, lg_ref, lb_ref, o_ref, bufp_ref, cv_ref,
                  *, width, hp, ta, rc, lc):
    c = u_ref.shape[1]
    t = pl.program_id(1)
    n_ext = hp + ta

    @pl.when(t == 0)
    def _():
        bufp_ref[0, 0:hp, :] = hist_ref[0]
        bufp_ref[0, n_ext:n_ext + SUBLANES, :] = jnp.zeros((SUBLANES, c), F32)

    bufp_ref[0, hp:hp + ta, :] = u_ref[...].astype(F32)

    def shift_body(r, carry):
        s0 = pl.multiple_of(r * rc, rc)
        win = bufp_ref[0, pl.ds(s0, rc + SUBLANES), :]
        for p in range(1, SUBLANES):
            bufp_ref[p, pl.ds(s0, rc), :] = win[p:p + rc, :]
        return carry

    lax.fori_loop(0, n_ext // rc, shift_body, 0)

    def conv_body(r, carry):
        r0 = pl.multiple_of(r * rc, rc)
        for l0 in range(0, c, lc):
            acc = jnp.zeros((rc, lc), F32)
            for k in range(width):
                off = hp - (width - 1) + k
                q8, p = (off // SUBLANES) * SUBLANES, off % SUBLANES
                xs = bufp_ref[p, pl.ds(r0 + q8, rc), l0:l0 + lc]
                acc = acc + xs * w_ref[k:k + 1, l0:l0 + lc]
            cv_ref[pl.ds(r0, rc), l0:l0 + lc] = acc
        return carry

    lax.fori_loop(0, ta // rc, conv_body, 0)
    bufp_ref[0, 0:hp, :] = bufp_ref[0, ta:ta + hp, :]

    v = cv_ref[...] + cb_ref[...]
    mu = jnp.mean(v, axis=-1, keepdims=True)
    vc = v - mu
    var = jnp.mean(vc * vc, axis=-1, keepdims=True)
    y = vc * lax.rsqrt(var + EPS) * lg_ref[...] + lb_ref[...]
    o_ref[...] = _silu(y).astype(o_ref.dtype)


def _conva_call(u, hist, w, cb, lg, lb, *, row0, nseq, seqlen, name):
    c = u.shape[1]
    width = w.shape[0]
    hp = hist.shape[1]
    ta = _pick(seqlen, (256, 128, 64))
    rc = 32
    lc = 256
    assert row0 % ta == 0 and (hp + ta) % rc == 0 and hp >= width - 1
    nt = seqlen // ta
    blk0 = row0 // ta
    const = lambda b, t: (0, 0)
    return pl.pallas_call(
        functools.partial(_conva_kernel, width=width, hp=hp, ta=ta, rc=rc, lc=lc),
        grid=(nseq, nt),
        in_specs=[pl.BlockSpec((ta, c), lambda b, t: (blk0 + b * nt + t, 0)),
                  pl.BlockSpec((1, hp, c), lambda b, t: (b, 0, 0)),
                  pl.BlockSpec((width, c), const),
                  pl.BlockSpec((1, c), const),
                  pl.BlockSpec((1, c), const),
                  pl.BlockSpec((1, c), const)],
        out_specs=pl.BlockSpec((ta, c), lambda b, t: (b * nt + t, 0)),
        out_shape=jax.ShapeDtypeStruct((nseq * seqlen, c), BF16),
        scratch_shapes=[pltpu.VMEM((SUBLANES, hp + ta + SUBLANES, c), F32), pltpu.VMEM((ta, c), F32)],
        compiler_params=_cparams(("arbitrary", "arbitrary")),
        name=name,
    )(u, hist, w, cb, lg, lb)


def _ssd_kernel(xbc_ref, zs_ref, dt_ref, s0_ref, a_ref, dsk_ref, ng_ref, ex_ref,
                y_ref, sout_ref, y3_ref, st_ref, cumt_ref, dtt_ref, wgt_ref,
                *, q, ngroups, hg, pdim, nstate):
    gw = hg * pdim
    d_inner = ngroups * gw
    c = pl.program_id(1)
    nchunks = pl.num_programs(1)

    @pl.when(c == 0)
    def _():
        st_ref[...] = s0_ref[0]

    dt = dt_ref[...]
    da = dt * a_ref[...]
    ii = lax.broadcasted_iota(jnp.int32, (q, q), 0)
    jj = lax.broadcasted_iota(jnp.int32, (q, q), 1)
    causal = ii >= jj
    tri = jnp.where(causal, 1.0, 0.0).astype(BF16)
    da_hi = da.astype(BF16)
    da_lo = (da - da_hi.astype(F32)).astype(BF16)
    cum = (jnp.dot(tri, da_hi, preferred_element_type=F32)
           + jnp.dot(tri, da_lo, preferred_element_type=F32))
    cum_t = cum.T
    dt_t = dt.T
    last_t = cum_t[:, q - 1:q]
    nh = ngroups * hg
    cumt_ref[...] = cum_t[0:nh, :].reshape(ngroups, hg, q)
    dtt_ref[...] = dt_t[0:nh, :].reshape(ngroups, hg, q)
    wgt_ref[...] = (dt_t * jnp.exp(last_t - cum_t))[0:nh, :].reshape(ngroups, hg, q)

    e_last = jnp.exp(cum[q - SUBLANES:q, :])
    e_hi = e_last.astype(BF16)
    e_lo = (e_last - e_hi.astype(F32)).astype(BF16)
    dec = (jnp.dot(e_hi, ex_ref[...], preferred_element_type=F32)
           + jnp.dot(e_lo, ex_ref[...], preferred_element_type=F32))
    dec_row = dec[SUBLANES - 1:SUBLANES, :]

    lane = lax.broadcasted_iota(jnp.int32, (1, LANES), 1)
    lo_mask = lane < pdim
    b0 = d_inner
    c0 = d_inner + ngroups * nstate

    for g in range(ngroups):
        bg = xbc_ref[:, b0 + g * nstate:b0 + (g + 1) * nstate]
        cg = xbc_ref[:, c0 + g * nstate:c0 + (g + 1) * nstate]
        scores = lax.dot_general(cg, bg, NT_DIMS, preferred_element_type=F32)
        bg_t = bg.astype(F32).T
        cg32 = cg.astype(F32)
        cum_g = cumt_ref[g]
        dt_g = dtt_ref[g]
        wg_g = wgt_ref[g]
        for hp2 in range(hg // 2):
            l0 = g * gw + hp2 * LANES
            xpair = xbc_ref[:, l0:l0 + LANES]
            spair = st_ref[g, :, hp2 * LANES:(hp2 + 1) * LANES]
            ypair = jnp.zeros((q, LANES), F32)
            upd = jnp.zeros((nstate, LANES), F32)
            for part in range(2):
                hh = 2 * hp2 + part
                mask = lo_mask if part == 0 else jnp.logical_not(lo_mask)
                x_h = jnp.where(mask, xpair, jnp.zeros_like(xpair))
                s_h = jnp.where(mask, spair, 0.0).astype(BF16)
                row = cum_g[hh:hh + 1, :]
                col = jnp.broadcast_to(row, (LANES, q)).T
                seg = col[:, 0:q] - row
                decay = jnp.exp(jnp.where(causal, seg, NEG_BIG))
                m_h = scores * decay * dt_g[hh:hh + 1, :]
                cs_h = cg32 * jnp.exp(col[:, 0:nstate])
                lhs = jnp.concatenate([m_h, cs_h], axis=1).astype(BF16)
                rhs = jnp.concatenate([x_h, s_h], axis=0)
                ypair = ypair + jnp.dot(lhs, rhs, preferred_element_type=F32)
                bw_t = (bg_t * wg_g[hh:hh + 1, :]).astype(BF16)
                upd = upd + jnp.dot(bw_t, x_h, preferred_element_type=F32)
            y3_ref[g, :, hp2 * LANES:(hp2 + 1) * LANES] = ypair
            st_ref[g, :, hp2 * LANES:(hp2 + 1) * LANES] = spair * dec_row[:, l0:l0 + LANES] + upd

    ssq = jnp.zeros((q, 1), F32)
    for g in range(ngroups):
        cols = slice(g * gw, (g + 1) * gw)
        yg = (y3_ref[g] + dsk_ref[:, cols] * xbc_ref[:, cols].astype(F32)) * zs_ref[:, cols].astype(F32)
        y3_ref[g] = yg
        ssq = ssq + jnp.sum(yg * yg, axis=-1, keepdims=True)
    inv = lax.rsqrt(ssq / d_inner + EPS)
    for g in range(ngroups):
        cols = slice(g * gw, (g + 1) * gw)
        y_ref[:, cols] = (y3_ref[g] * inv * ng_ref[:, cols]).astype(y_ref.dtype)

    @pl.when(c == nchunks - 1)
    def _():
        sout_ref[0] = st_ref[...]


def _ssd_call(xbc, zs, dt, s0, a_row, dsk_row, ng_row, ex, *,
              row0, nseq, seqlen, ngroups, hg, pdim, nstate, name):
    dxbc = xbc.shape[1]
    d_inner = zs.shape[1]
    q = _pick(seqlen, (SSD_CHUNK, 64))
    assert row0 % q == 0 and pdim * 2 == LANES and hg % 2 == 0 and nstate == LANES
    nc = seqlen // q
    blk0 = row0 // q
    gw = hg * pdim
    row_map = lambda b, c: (blk0 + b * nc + c, 0)
    const = lambda b, c: (0, 0)
    seq4 = lambda b, c: (b, 0, 0, 0)
    return pl.pallas_call(
        functools.partial(_ssd_kernel, q=q, ngroups=ngroups, hg=hg, pdim=pdim, nstate=nstate),
        grid=(nseq, nc),
        in_specs=[pl.BlockSpec((q, dxbc), row_map),
                  pl.BlockSpec((q, d_inner), row_map),
                  pl.BlockSpec((q, LANES), row_map),
                  pl.BlockSpec((1, ngroups, nstate, gw), seq4),
                  pl.BlockSpec((1, LANES), const),
                  pl.BlockSpec((1, d_inner), const),
                  pl.BlockSpec((1, d_inner), const),
                  pl.BlockSpec((LANES, d_inner), const)],
        out_specs=[pl.BlockSpec((q, d_inner), lambda b, c: (b * nc + c, 0)),
                   pl.BlockSpec((1, ngroups, nstate, gw), seq4)],
        out_shape=[jax.ShapeDtypeStruct((nseq * seqlen, d_inner), BF16),
                   jax.ShapeDtypeStruct((nseq, ngroups, nstate, gw), F32)],
        scratch_shapes=[pltpu.VMEM((ngroups, q, gw), F32),
                        pltpu.VMEM((ngroups, nstate, gw), F32),
                        pltpu.VMEM((ngroups, hg, q), F32),
                        pltpu.VMEM((ngroups, hg, q), F32),
                        pltpu.VMEM((ngroups, hg, q), F32)],
        compiler_params=_cparams(("arbitrary", "arbitrary")),
        name=name,
    )(xbc, zs, dt, s0, a_row, dsk_row, ng_row, ex)


def _pad_hist(cache, hp):
    nseq, wm1, c = cache.shape
    return jnp.pad(cache.astype(F32), ((0, 0), (hp - wm1, 0), (0, 0)))


def _tail_rows(rows_all, row0, nseq, seqlen, keep):
    return jnp.stack([lax.slice_in_dim(rows_all, row0 + (b + 1) * seqlen - keep, row0 + (b + 1) * seqlen)
                      for b in range(nseq)]).astype(F32)


def _conv_tails(tail, keep):
    nseq, nw, _, c = tail.shape
    return tail[:, :, SUBLANES - keep:].transpose(0, 2, 1, 3).reshape(nseq, keep, nw * c)


def _layer(xp, xs, segs, cache_s, wts):
    (norm_mix_pre, w_in, b_gate, conv_a_w, conv_a_b, ln_a_g, ln_a_b, w_a_out, conv_b_w, conv_b_b,
     dt_bias, a_log, d_skip, ssd_norm_g, w_b_out, w_o, norm_mix_post, norm_ffn_pre, w_up,
     ffn_conv_w, ffn_conv_b, w_down, norm_ffn_post) = wts
    tp, d = xp.shape
    ts = xs.shape[0]
    (_, nseq_p, len_p), (_, nseq_s, len_s) = segs
    dc = conv_a_w.shape[1]
    wa = conv_a_w.shape[0]
    dxbc = conv_b_w.shape[1]
    wb = conv_b_w.shape[0]
    nheads = a_log.shape[0]
    d_inner = ssd_norm_g.shape[0]
    pdim = d_inner // nheads
    nstate = cache_s[2].shape[-1]
    ngroups = (dxbc - d_inner) // (2 * nstate)
    hg = nheads // ngroups
    gw = hg * pdim
    f2 = ffn_conv_w.shape[1]
    dff = f2 // 2
    wf = ffn_conv_w.shape[0]
    assert nheads <= LANES and min(len_p, len_s) >= max(wa, wb, wf) - 1

    row = lambda v: v.reshape(1, -1).astype(F32)
    tm = _pick(ts, (1024, 512, 256, 128, 64))
    assert tp % tm == 0 and len_p % tm == 0
    n_p = tp // tm
    seq_kw = dict(tm=tm, n_p=n_p, tiles_per_seq=len_p // tm, nseq_p=nseq_p, ls=len_s)

    h = _prenorm(xp, xs, row(norm_mix_pre), tm=tm)

    w_in_t = jnp.swapaxes(w_in, 0, 1)
    o_val, o_gate, o_z, o_xbc = 0, dc, 2 * dc, 2 * dc + d_inner
    o_dt = o_xbc + dxbc
    o_g = o_dt + nheads
    bh = _pick(dc, (512, 256, 128))
    u = _matmul(h, [(w_in_t, bh, o_val // bh), (w_in_t, bh, o_gate // bh)], nj=dc // bh, order="col", tm=tm,
                outs=[(dc, bh, BF16, False)], epilogue=_ep_glu, cast_w=True, w_t=True, name="inproj_glu")
    bz = _pick(d_inner, (1024, 512, 256, 128))
    assert o_z % bz == 0
    zs = _matmul(h, [(w_in_t, bz, o_z // bz)], nj=d_inner // bz, order="col", tm=tm,
                 outs=[(d_inner, bz, BF16, False)], epilogue=_ep_silu, cast_w=True, w_t=True, name="inproj_z")
    bx = _pick(dxbc, (1024, 512, 256, 128))
    assert o_xbc % bx == 0
    xbc_act, xbc_tail_p, xbc_tail_s = _proj_conv(
        h, w_in_t, [o_xbc // bx], conv_b_w.astype(F32), row(conv_b_b), _pad_hist(cache_s[1], SUBLANES), [0],
        nj=dxbc // bx, bw=bx, w_t=True, act_fn=_act_silu, name="inproj_xbc", **seq_kw)
    assert o_dt % LANES == 0 and o_dt + LANES <= w_in_t.shape[0]
    dt_b = jnp.pad(dt_bias.astype(F32), (0, LANES - nheads)).reshape(1, LANES)
    dt = _matmul(h, [(w_in_t, LANES, o_dt // LANES)], nj=1, order="col", tm=tm,
                 outs=[(LANES, LANES, F32, False)], epilogue=functools.partial(_ep_dt, nheads=nheads),
                 rows=[(dt_b, LANES, 0)], cast_w=True, w_t=True, name="inproj_dt")
    bg = _pick(2 * d, (1024, 512, 256, 128))
    gates = _matmul(h, [(w_in_t[o_g:].astype(BF16), bg, 0)], nj=2 * d // bg, order="col", tm=tm,
                    outs=[(2 * d, bg, BF16, False)], epilogue=_ep_bias_sigmoid, w_t=True,
                    rows=[(row(b_gate), bg, 0)], name="inproj_gates")

    hp_a = -(-(wa - 1) // SUBLANES) * SUBLANES
    hist_a = [jnp.zeros((nseq_p, hp_a, dc), F32), _pad_hist(cache_s[0], hp_a)]
    a_act = tuple(_conva_call(u, hist_a[si], conv_a_w.astype(F32), row(conv_a_b), row(ln_a_g), row(ln_a_b),
                              row0=row0, nseq=nseq, seqlen=seqlen, name=f"conva_{si}")
                  for si, (row0, nseq, seqlen) in enumerate(segs))
    bd = _pick(d, (1024, 512, 256, 128))
    ya = _matmul(a_act, [(w_a_out.astype(BF16), bd, 0)], nj=d // bd, order="row", tm=tm, n_p=n_p,
                 outs=[(d, bd, BF16, False)], epilogue=_ep_gate, blks=[(gates, bd, 0)], name="proj_a_out")

    a_row = jnp.pad(-jnp.exp(a_log.astype(F32)), (0, LANES - nheads)).reshape(1, LANES)
    dsk_row = jnp.repeat(d_skip.astype(F32), pdim).reshape(1, d_inner)
    ex = (jnp.arange(LANES)[:, None] == (jnp.arange(d_inner)[None, :] // pdim)).astype(BF16)
    s0 = [jnp.zeros((nseq_p, ngroups, nstate, gw), F32),
          cache_s[2].astype(F32).reshape(nseq_s, ngroups, gw, nstate).transpose(0, 1, 3, 2)]
    yn, new_states = [], []
    for si, (row0, nseq, seqlen) in enumerate(segs):
        y_si, s_new = _ssd_call(xbc_act, zs, dt, s0[si], a_row, dsk_row, row(ssd_norm_g), ex,
                                row0=row0, nseq=nseq, seqlen=seqlen, ngroups=ngroups, hg=hg,
                                pdim=pdim, nstate=nstate, name=f"ssd_{si}")
        yn.append(y_si)
        new_states.append(s_new.transpose(0, 1, 3, 2).reshape(nseq, nheads, pdim, nstate))
    tm_b = _pick(tm, (512, 256, 128, 64))
    merged = _matmul(tuple(yn), [(w_b_out.astype(BF16), bd, 0)], nj=d // bd, order="row", tm=tm_b,
                     n_p=tp // tm_b, outs=[(d, bd, BF16, False)], epilogue=_ep_gate_add,
                     blks=[(gates, bd, d // bd), (ya, bd, 0)], name="proj_b_out")
    tm_o = _pick(tm, (256, 128, 64))
    x1, h2 = _matmul(merged, [(w_o.astype(BF16), d, 0)], nj=1, order="row", tm=tm_o,
                     outs=[(d, d, F32, False), (d, d, BF16, False)], epilogue=_ep_residual_norm,
                     rows=[(row(norm_mix_post), d, 0), (row(norm_ffn_pre), d, 0)],
                     blks=[((xp, xs), d, 0)], n_p=tp // tm_o, single_buffer_w=True, name="proj_o")

    bf = _pick(dff, (512, 256, 128))
    act, ffn_tail_p, ffn_tail_s = _proj_conv(
        h2, w_up, [0, dff // bf], ffn_conv_w.astype(F32), row(ffn_conv_b), _pad_hist(cache_s[3], SUBLANES),
        [0, dff // bf], nj=dff // bf, bw=bf, w_t=False, act_fn=_act_gelu_gate, name="ffn_up", **seq_kw)
    yp, ys = _matmul(act, [(w_down.astype(BF16), d, 0)], nj=1, order="row", tm=tm_o,
                     outs=[(d, d, F32, True)], epilogue=_ep_residual,
                     rows=[(row(norm_ffn_post), d, 0)], blks=[(x1, d, 0)], n_p=tp // tm_o,
                     single_buffer_w=True, name="ffn_down")

    new_caches = []
    for si, (row0, nseq, seqlen) in enumerate(segs):
        new_caches.append((_tail_rows(u, row0, nseq, seqlen, wa - 1),
                           _conv_tails((xbc_tail_p, xbc_tail_s)[si], wb - 1),
                           new_states[si],
                           _conv_tails((ffn_tail_p, ffn_tail_s)[si], wf - 1)))
    return yp, ys, new_caches


def kernel(x_prompt, x_sample, cache_conv_a, cache_conv_b, state_ssd, cache_ffn_conv, norm_mix_pre, w_in, b_gate, conv_a_w, conv_a_b, ln_a_g, ln_a_b, w_a_out, conv_b_w, conv_b_b, dt_bias, a_log, d_skip, ssd_norm_g, w_b_out, w_o, norm_mix_post, norm_ffn_pre, w_up, ffn_conv_w, ffn_conv_b, w_down, norm_ffn_post):
    weights = (norm_mix_pre, w_in, b_gate, conv_a_w, conv_a_b, ln_a_g, ln_a_b, w_a_out, conv_b_w,
               conv_b_b, dt_bias, a_log, d_skip, ssd_norm_g, w_b_out, w_o, norm_mix_post,
               norm_ffn_pre, w_up, ffn_conv_w, ffn_conv_b, w_down, norm_ffn_post)
    depth = w_in.shape[0]
    bp, lp, d = x_prompt.shape
    bs, ls, _ = x_sample.shape
    tp = bp * lp
    segs = ((0, bp, lp), (tp, bs, ls))
    xp = x_prompt.reshape(tp, d)
    xs = x_sample.reshape(bs * ls, d)
    dt_in = x_prompt.dtype
    outs_p = ([], [], [], [])
    outs_s = ([], [], [], [])
    for layer in range(depth):
        wl = tuple(w[layer] for w in weights)
        cache_s = (cache_conv_a[layer], cache_conv_b[layer], state_ssd[layer], cache_ffn_conv[layer])
        xp, xs, (new_p, new_s) = _layer(xp, xs, segs, cache_s, wl)
        for lst, v in zip(outs_p, new_p):
            lst.append(v.astype(dt_in))
        for lst, v in zip(outs_s, new_s):
            lst.append(v.astype(dt_in))
    return (xp.reshape(bp, lp, d), xs.reshape(bs, ls, d),
            jnp.stack(outs_p[0]), jnp.stack(outs_p[1]), jnp.stack(outs_p[2]), jnp.stack(outs_p[3]),
            jnp.stack(outs_s[0]), jnp.stack(outs_s[1]), jnp.stack(outs_s[2]), jnp.stack(outs_s[3]))
```

```python
import functools

import jax
import jax.numpy as jnp
from jax import lax
from jax.experimental import pallas as pl
from jax.experimental.pallas import tpu as pltpu

F32 = jnp.float32
BF16 = jnp.bfloat16
EPS = 1e-6
LANES = 128
SUBLANES = 8
VMEM_LIMIT = 56 * 1024 * 1024
SSD_CHUNK = 128
NEG_BIG = -1e30
NN_DIMS = (((1,), (0,)), ((), ()))
NT_DIMS = (((1,), (1,)), ((), ()))


def _pick(n, cands):
    for c in cands:
        if n % c == 0:
            return c
    raise ValueError(f"no tile in {cands} divides {n}")


def _cparams(sem):
    return pltpu.CompilerParams(dimension_semantics=sem, vmem_limit_bytes=VMEM_LIMIT)


def _sigmoid(x):
    return 1.0 / (1.0 + jnp.exp(-x))


def _silu(x):
    return x * _sigmoid(x)


def _softplus(x):
    return jnp.maximum(x, 0.0) + jnp.log1p(jnp.exp(-jnp.abs(x)))


def _rms_scale(x, eps):
    return x * lax.rsqrt(jnp.mean(x * x, axis=-1, keepdims=True) + eps)


def _prenorm_kernel(xp_ref, xs_ref, g_ref, o_ref, *, n_p):
    i = pl.program_id(0)

    def run(x_ref):
        o_ref[...] = (_rms_scale(x_ref[...].astype(F32), EPS) * g_ref[...]).astype(o_ref.dtype)

    pl.when(i < n_p)(functools.partial(run, xp_ref))
    pl.when(i >= n_p)(functools.partial(run, xs_ref))


def _prenorm(xp, xs, gain, *, tm):
    tp, d = xp.shape
    ts = xs.shape[0]
    n_p = tp // tm
    return pl.pallas_call(
        functools.partial(_prenorm_kernel, n_p=n_p),
        grid=((tp + ts) // tm,),
        in_specs=[pl.BlockSpec((tm, d), lambda i: (jnp.minimum(i, n_p - 1), 0)),
                  pl.BlockSpec((tm, d), lambda i: (jnp.maximum(i - n_p, 0), 0)),
                  pl.BlockSpec((1, d), lambda i: (0, 0))],
        out_specs=pl.BlockSpec((tm, d), lambda i: (i, 0)),
        out_shape=jax.ShapeDtypeStruct((tp + ts, d), BF16),
        compiler_params=_cparams(("arbitrary",)),
        name="prenorm",
    )(xp, xs, gain)


def _mm_kernel(*refs, order, n_w, cast_w, w_t, lhs_dual, n_rows, blk_dual, out_dual, n_p, epilogue):
    it = iter(refs)
    lhs_refs = tuple(next(it) for _ in range(2 if lhs_dual else 1))
    w_refs = [next(it) for _ in range(n_w)]
    row_refs = [next(it) for _ in range(n_rows)]
    blk_refs = [tuple(next(it) for _ in range(2 if d else 1)) for d in blk_dual]
    out_refs = [tuple(next(it) for _ in range(2 if d else 1)) for d in out_dual]
    i = pl.program_id(1 if order == "col" else 0)
    if cast_w:
        wbuf = next(it)

        @pl.when(i == 0)
        def _():
            c0 = 0
            for wr in w_refs:
                if w_t:
                    wbuf[c0:c0 + wr.shape[0], :] = wr[...].astype(BF16)
                    c0 += wr.shape[0]
                else:
                    wbuf[:, c0:c0 + wr.shape[1]] = wr[...].astype(BF16)
                    c0 += wr.shape[1]

        w_ref = wbuf
    else:
        w_ref = w_refs[0]

    def run(sel):
        pick = lambda pair: pair[sel if len(pair) > 1 else 0]
        r = lax.dot_general(pick(lhs_refs)[...], w_ref[...], NT_DIMS if w_t else NN_DIMS,
                            preferred_element_type=F32)
        epilogue(r, row_refs, [pick(b) for b in blk_refs], [pick(o) for o in out_refs])

    if lhs_dual or any(blk_dual) or any(out_dual):
        pl.when(i < n_p)(functools.partial(run, 0))
        pl.when(i >= n_p)(functools.partial(run, 1))
    else:
        run(0)


def _matmul(lhs, ws, *, nj, order, tm, outs, epilogue, name, rows=(), blks=(), cast_w=False,
            w_t=False, n_p=None, single_buffer_w=False):
    lhs_dual = isinstance(lhs, tuple)
    t = lhs[0].shape[0] + lhs[1].shape[0] if lhs_dual else lhs.shape[0]
    k = lhs[0].shape[1] if lhs_dual else lhs.shape[1]
    ni = t // tm
    if order == "col":
        grid = (nj, ni)
        spec = lambda shape, fn, **kw: pl.BlockSpec(shape, lambda a, b: fn(b, a), **kw)
    else:
        grid = (ni, nj)
        spec = lambda shape, fn, **kw: pl.BlockSpec(shape, lambda a, b: fn(a, b), **kw)
    assert not cast_w or order == "col" or nj == 1
    prompt_row = lambda i: jnp.minimum(i, n_p - 1)
    decode_row = lambda i: jnp.maximum(i - n_p, 0)

    if lhs_dual:
        in_specs = [spec((tm, k), lambda i, j: (prompt_row(i), 0)),
                    spec((tm, k), lambda i, j: (decode_row(i), 0))]
        args = list(lhs)
    else:
        in_specs = [spec((tm, k), lambda i, j: (i, 0))]
        args = [lhs]
    w_kw = {"pipeline_mode": pl.Buffered(1)} if single_buffer_w else {}
    bn = 0
    for arr, bw, off in ws:
        if w_t:
            in_specs.append(spec((bw, k), lambda i, j, off=off: (j + off, 0), **w_kw))
        else:
            in_specs.append(spec((k, bw), lambda i, j, off=off: (0, j + off), **w_kw))
        args.append(arr)
        bn += bw
    for arr, bw, off in rows:
        in_specs.append(spec((1, bw), lambda i, j, off=off: (0, j + off)))
        args.append(arr)
    for arr, bw, off in blks:
        if isinstance(arr, tuple):
            in_specs.append(spec((tm, bw), lambda i, j, off=off: (prompt_row(i), j + off)))
            in_specs.append(spec((tm, bw), lambda i, j, off=off: (decode_row(i), j + off)))
            args.extend(arr)
        else:
            in_specs.append(spec((tm, bw), lambda i, j, off=off: (i, j + off)))
            args.append(arr)
    out_specs, out_shape = [], []
    for cols, bw, dtype, dual in outs:
        if dual:
            out_specs.append(spec((tm, bw), lambda i, j: (prompt_row(i), j)))
            out_specs.append(spec((tm, bw), lambda i, j: (decode_row(i), j)))
            out_shape.append(jax.ShapeDtypeStruct((n_p * tm, cols), dtype))
            out_shape.append(jax.ShapeDtypeStruct(((ni - n_p) * tm, cols), dtype))
        else:
            out_specs.append(spec((tm, bw), lambda i, j: (i, j)))
            out_shape.append(jax.ShapeDtypeStruct((t, cols), dtype))
    scratch = [pltpu.VMEM((bn, k) if w_t else (k, bn), BF16)] if cast_w else []
    res = pl.pallas_call(
        functools.partial(_mm_kernel, order=order, n_w=len(ws), cast_w=cast_w, w_t=w_t,
                          lhs_dual=lhs_dual, n_rows=len(rows),
                          blk_dual=[isinstance(b[0], tuple) for b in blks],
                          out_dual=[o[3] for o in outs], n_p=n_p, epilogue=epilogue),
        grid=grid,
        in_specs=in_specs,
        out_specs=out_specs,
        out_shape=out_shape,
        scratch_shapes=scratch,
        compiler_params=_cparams(("arbitrary", "arbitrary")),
        name=name,
    )(*args)
    return res[0] if len(res) == 1 else res


def _ep_glu(r, rows, blks, outs):
    half = r.shape[1] // 2
    outs[0][...] = (r[:, :half] * _sigmoid(r[:, half:])).astype(outs[0].dtype)


def _ep_silu(r, rows, blks, outs):
    outs[0][...] = _silu(r).astype(outs[0].dtype)


def _ep_bias_sigmoid(r, rows, blks, outs):
    outs[0][...] = _sigmoid(r + rows[0][...]).astype(outs[0].dtype)


def _ep_dt(r, rows, blks, outs, *, nheads):
    lane = lax.broadcasted_iota(jnp.int32, r.shape, 1)
    outs[0][...] = jnp.where(lane < nheads, _softplus(r + rows[0][...]), 0.0).astype(outs[0].dtype)


def _ep_gate(r, rows, blks, outs):
    outs[0][...] = (r * blks[0][...].astype(F32)).astype(outs[0].dtype)


def _ep_gate_add(r, rows, blks, outs):
    outs[0][...] = (r * blks[0][...].astype(F32) + blks[1][...].astype(F32)).astype(outs[0].dtype)


def _ep_residual_norm(r, rows, blks, outs):
    x1 = blks[0][...] + _rms_scale(r, EPS) * rows[0][...]
    outs[0][...] = x1
    outs[1][...] = (_rms_scale(x1, EPS) * rows[1][...]).astype(outs[1].dtype)


def _ep_residual(r, rows, blks, outs):
    outs[0][...] = blks[0][...] + _rms_scale(r, EPS) * rows[0][...]


def _proj_conv_kernel(*refs, nw, w_t, width, tm, bw, sm, n_p, tiles_per_seq, ls, act_fn):
    it = iter(refs)
    h_ref = next(it)
    w_refs = [next(it) for _ in range(nw)]
    cw_refs = [next(it) for _ in range(nw)]
    cb_refs = [next(it) for _ in range(nw)]
    hist_refs = [next(it) for _ in range(nw)]
    act_ref, tailp_ref, tails_ref, wbuf, pbuf = (next(it) for _ in range(5))
    hp = SUBLANES
    i = pl.program_id(1)

    @pl.when(i == 0)
    def _():
        for m, wr in enumerate(w_refs):
            wm = wr[...].T if w_t else wr[...]
            wbuf[:, m * bw:(m + 1) * bw] = wm.astype(BF16)

    def project(r0):
        return jnp.dot(h_ref[r0:r0 + sm, :], wbuf[...], preferred_element_type=F32)

    def conv_act(prev, r):
        ext = jnp.concatenate([prev, r], axis=0)
        convs = []
        for m in range(nw):
            e = ext[:, m * bw:(m + 1) * bw]
            acc = e * cw_refs[m][0:1, :]
            for k in range(1, width):
                acc = e * cw_refs[m][k:k + 1, :] + pltpu.roll(acc, 1, axis=0)
            convs.append(acc[hp:, :] + cb_refs[m][...])
        return act_fn(convs).astype(act_ref.dtype)

    @pl.when(i < n_p)
    def _():
        @pl.when(i % tiles_per_seq == 0)
        def _():
            pbuf[...] = jnp.zeros(pbuf.shape, F32)

        for s in range(tm // sm):
            r = project(s * sm)
            act_ref[s * sm:(s + 1) * sm, :] = conv_act(pbuf[...], r)
            pbuf[...] = r[sm - hp:sm, :]

        @pl.when(i % tiles_per_seq == tiles_per_seq - 1)
        def _():
            for m in range(nw):
                tailp_ref[0, m] = pbuf[:, m * bw:(m + 1) * bw]

    @pl.when(i >= n_p)
    def _():
        for s in range(tm // sm):
            r = project(s * sm)
            for q in range(sm // ls):
                seq = s * (sm // ls) + q
                rq = r[q * ls:(q + 1) * ls, :]
                for m in range(nw):
                    tails_ref[seq, m] = rq[ls - hp:ls, m * bw:(m + 1) * bw]
                prev = jnp.concatenate([hist_refs[m][seq] for m in range(nw)], axis=1)
                act_ref[s * sm + q * ls:s * sm + (q + 1) * ls, :] = conv_act(prev, rq)


def _proj_conv(h, w, w_offs, conv_w, conv_b, hist_s, c_offs, *, nj, bw, w_t, act_fn, tm, n_p,
               tiles_per_seq, nseq_p, ls, name):
    t, k = h.shape
    nw = len(w_offs)
    width = conv_w.shape[0]
    ni = t // tm
    sm = _pick(tm, (256, 128, 64))
    spt = tm // ls
    nseq_s = hist_s.shape[0]
    assert tm % ls == 0 and sm % ls == 0 and ls >= SUBLANES and width - 1 <= SUBLANES
    decode_tile = lambda i: jnp.maximum(i - n_p, 0)
    prompt_seq = lambda i: jnp.minimum(i, n_p - 1) // tiles_per_seq
    in_specs = [pl.BlockSpec((tm, k), lambda j, i: (i, 0))]
    args = [h]
    for off in w_offs:
        if w_t:
            in_specs.append(pl.BlockSpec((bw, k), lambda j, i, off=off: (j + off, 0)))
        else:
            in_specs.append(pl.BlockSpec((k, bw), lambda j, i, off=off: (0, j + off)))
        args.append(w)
    for off in c_offs:
        in_specs.append(pl.BlockSpec((width, bw), lambda j, i, off=off: (0, j + off)))
        args.append(conv_w)
    for off in c_offs:
        in_specs.append(pl.BlockSpec((1, bw), lambda j, i, off=off: (0, j + off)))
        args.append(conv_b)
    for off in c_offs:
        in_specs.append(pl.BlockSpec((spt, SUBLANES, bw), lambda j, i, off=off: (decode_tile(i), 0, j + off)))
        args.append(hist_s)
    out_specs = [pl.BlockSpec((tm, bw), lambda j, i: (i, j)),
                 pl.BlockSpec((1, nw, SUBLANES, bw), lambda j, i: (prompt_seq(i), 0, 0, j)),
                 pl.BlockSpec((spt, nw, SUBLANES, bw), lambda j, i: (decode_tile(i), 0, 0, j))]
    out_shape = [jax.ShapeDtypeStruct((t, nj * bw), BF16),
                 jax.ShapeDtypeStruct((nseq_p, nw, SUBLANES, nj * bw), F32),
                 jax.ShapeDtypeStruct((nseq_s, nw, SUBLANES, nj * bw), F32)]
    return pl.pallas_call(
        functools.partial(_proj_conv_kernel, nw=nw, w_t=w_t, width=width, tm=tm, bw=bw, sm=sm, n_p=n_p,
                          tiles_per_seq=tiles_per_seq, ls=ls, act_fn=act_fn),
        grid=(nj, ni),
        in_specs=in_specs,
        out_specs=out_specs,
        out_shape=out_shape,
        scratch_shapes=[pltpu.VMEM((k, nw * bw), BF16), pltpu.VMEM((SUBLANES, nw * bw), F32)],
        compiler_params=_cparams(("arbitrary", "arbitrary")),
        name=name,
    )(*args)


def _act_silu(convs):
    return 0.5 * convs[0] * (1.0 + jnp.tanh(0.5 * convs[0]))


def _act_gelu_gate(convs):
    return jax.nn.gelu(convs[0]) * convs[1]


def _conva_kernel(u_ref, hist_ref, w_ref, cb_ref, lg_ref, lb_ref, o_ref, bufp_ref, cv_ref,
                  *, width, hp, ta, rc, lc):
    c = u_ref.shape[1]
    t = pl.program_id(1)
    n_ext = hp + ta

    @pl.when(t == 0)
    def _():
        bufp_ref[0, 0:hp, :] = hist_ref[0]
        bufp_ref[0, n_ext:n_ext + SUBLANES, :] = jnp.zeros((SUBLANES, c), F32)

    bufp_ref[0, hp:hp + ta, :] = u_ref[...].astype(F32)

    def shift_body(r, carry):
        s0 = pl.multiple_of(r * rc, rc)
        win = bufp_ref[0, pl.ds(s0, rc + SUBLANES), :]
        for p in range(1, SUBLANES):
            bufp_ref[p, pl.ds(s0, rc), :] = win[p:p + rc, :]
        return carry

    lax.fori_loop(0, n_ext // rc, shift_body, 0)

    def conv_body(r, carry):
        r0 = pl.multiple_of(r * rc, rc)
        for l0 in range(0, c, lc):
            acc = jnp.zeros((rc, lc), F32)
            for k in range(width):
                off = hp - (width - 1) + k
                q8, p = (off // SUBLANES) * SUBLANES, off % SUBLANES
                xs = bufp_ref[p, pl.ds(r0 + q8, rc), l0:l0 + lc]
                acc = acc + xs * w_ref[k:k + 1, l0:l0 + lc]
            cv_ref[pl.ds(r0, rc), l0:l0 + lc] = acc
        return carry

    lax.fori_loop(0, ta // rc, conv_body, 0)
    bufp_ref[0, 0:hp, :] = bufp_ref[0, ta:ta + hp, :]

    v = cv_ref[...] + cb_ref[...]
    mu = jnp.mean(v, axis=-1, keepdims=True)
    vc = v - mu
    var = jnp.mean(vc * vc, axis=-1, keepdims=True)
    y = vc * lax.rsqrt(var + EPS) * lg_ref[...] + lb_ref[...]
    o_ref[...] = _silu(y).astype(o_ref.dtype)


def _conva_call(u, hist, w, cb, lg, lb, *, row0, nseq, seqlen, name):
    c = u.shape[1]
    width = w.shape[0]
    hp = hist.shape[1]
    ta = _pick(seqlen, (256, 128, 64))
    rc = 32
    lc = 256
    assert row0 % ta == 0 and (hp + ta) % rc == 0 and hp >= width - 1
    nt = seqlen // ta
    blk0 = row0 // ta
    const = lambda b, t: (0, 0)
    return pl.pallas_call(
        functools.partial(_conva_kernel, width=width, hp=hp, ta=ta, rc=rc, lc=lc),
        grid=(nseq, nt),
        in_specs=[pl.BlockSpec((ta, c), lambda b, t: (blk0 + b * nt + t, 0)),
                  pl.BlockSpec((1, hp, c), lambda b, t: (b, 0, 0)),
                  pl.BlockSpec((width, c), const),
                  pl.BlockSpec((1, c), const),
                  pl.BlockSpec((1, c), const),
                  pl.BlockSpec((1, c), const)],
        out_specs=pl.BlockSpec((ta, c), lambda b, t: (b * nt + t, 0)),
        out_shape=jax.ShapeDtypeStruct((nseq * seqlen, c), BF16),
        scratch_shapes=[pltpu.VMEM((SUBLANES, hp + ta + SUBLANES, c), F32), pltpu.VMEM((ta, c), F32)],
        compiler_params=_cparams(("arbitrary", "arbitrary")),
        name=name,
    )(u, hist, w, cb, lg, lb)


def _ssd_kernel(xbc_ref, zs_ref, dt_ref, s0_ref, a_ref, dsk_ref, ng_ref, ex_ref,
                y_ref, sout_ref, y3_ref, st_ref, cumt_ref, cml_ref, wgt_ref,
                *, q, ngroups, hg, pdim, nstate):
    gw = hg * pdim
    d_inner = ngroups * gw
    c = pl.program_id(1)
    nchunks = pl.num_programs(1)

    @pl.when(c == 0)
    def _():
        st_ref[...] = s0_ref[0]

    dt = dt_ref[...]
    da = dt * a_ref[...]
    ii = lax.broadcasted_iota(jnp.int32, (q, q), 0)
    jj = lax.broadcasted_iota(jnp.int32, (q, q), 1)
    causal = ii >= jj
    tri = jnp.where(causal, 1.0, 0.0).astype(BF16)
    da_hi = da.astype(BF16)
    da_lo = (da - da_hi.astype(F32)).astype(BF16)
    cum = (jnp.dot(tri, da_hi, preferred_element_type=F32)
           + jnp.dot(tri, da_lo, preferred_element_type=F32))
    cum_t = cum.T
    dt_t = dt.T
    last_t = cum_t[:, q - 1:q]
    nh = ngroups * hg
    cumt_ref[...] = cum_t[0:nh, :].reshape(ngroups, hg, q)
    cml_ref[...] = (cum_t - jnp.log(dt_t))[0:nh, :].reshape(ngroups, hg, q)
    wgt_ref[...] = (dt_t * jnp.exp(last_t - cum_t))[0:nh, :].reshape(ngroups, hg, q)

    e_last = jnp.exp(cum[q - SUBLANES:q, :])
    e_hi = e_last.astype(BF16)
    e_lo = (e_last - e_hi.astype(F32)).astype(BF16)
    dec = (jnp.dot(e_hi, ex_ref[...], preferred_element_type=F32)
           + jnp.dot(e_lo, ex_ref[...], preferred_element_type=F32))
    dec_row = dec[SUBLANES - 1:SUBLANES, :]

    lane = lax.broadcasted_iota(jnp.int32, (1, LANES), 1)
    lo_mask = lane < pdim
    b0 = d_inner
    c0 = d_inner + ngroups * nstate

    for g in range(ngroups):
        bg = xbc_ref[:, b0 + g * nstate:b0 + (g + 1) * nstate]
        cg = xbc_ref[:, c0 + g * nstate:c0 + (g + 1) * nstate]
        scores = lax.dot_general(cg, bg, NT_DIMS, preferred_element_type=F32)
        bg_t = bg.astype(F32).T
        cg32 = cg.astype(F32)
        cum_g = cumt_ref[g]
        cml_g = cml_ref[g]
        wg_g = wgt_ref[g]
        for hp2 in range(hg // 2):
            l0 = g * gw + hp2 * LANES
            xpair = xbc_ref[:, l0:l0 + LANES]
            spair = st_ref[g, :, hp2 * LANES:(hp2 + 1) * LANES]
            lhs_parts, rhs_parts, bw_parts, x_parts = [], [], [], []
            for part in range(2):
                hh = 2 * hp2 + part
                mask = lo_mask if part == 0 else jnp.logical_not(lo_mask)
                x_h = jnp.where(mask, xpair, jnp.zeros_like(xpair))
                s_h = jnp.where(mask, spair, 0.0).astype(BF16)
                row = cum_g[hh:hh + 1, :]
                col = jnp.broadcast_to(row, (LANES, q)).T
                seg = col[:, 0:q] - cml_g[hh:hh + 1, :]
                m_h = scores * jnp.exp(jnp.where(causal, seg, NEG_BIG))
                cs_h = cg32 * jnp.exp(col[:, 0:nstate])
                lhs_parts += [m_h.astype(BF16), cs_h.astype(BF16)]
                rhs_parts += [x_h, s_h]
                bw_parts.append((bg_t * wg_g[hh:hh + 1, :]).astype(BF16))
                x_parts.append(x_h)
            ypair = jnp.dot(jnp.concatenate(lhs_parts, axis=1), jnp.concatenate(rhs_parts, axis=0),
                            preferred_element_type=F32)
            upd = jnp.dot(jnp.concatenate(bw_parts, axis=1), jnp.concatenate(x_parts, axis=0),
                          preferred_element_type=F32)
            y3_ref[g, :, hp2 * LANES:(hp2 + 1) * LANES] = ypair
            st_ref[g, :, hp2 * LANES:(hp2 + 1) * LANES] = spair * dec_row[:, l0:l0 + LANES] + upd

    ssq = jnp.zeros((q, 1), F32)
    for g in range(ngroups):
        cols = slice(g * gw, (g + 1) * gw)
        yg = (y3_ref[g] + dsk_ref[:, cols] * xbc_ref[:, cols].astype(F32)) * zs_ref[:, cols].astype(F32)
        y3_ref[g] = yg
        ssq = ssq + jnp.sum(yg * yg, axis=-1, keepdims=True)
    inv = lax.rsqrt(ssq / d_inner + EPS)
    for g in range(ngroups):
        cols = slice(g * gw, (g + 1) * gw)
        y_ref[:, cols] = (y3_ref[g] * inv * ng_ref[:, cols]).astype(y_ref.dtype)

    @pl.when(c == nchunks - 1)
    def _():
        sout_ref[0] = st_ref[...]


def _ssd_call(xbc, zs, dt, s0, a_row, dsk_row, ng_row, ex, *,
              row0, nseq, seqlen, ngroups, hg, pdim, nstate, name):
    dxbc = xbc.shape[1]
    d_inner = zs.shape[1]
    q = _pick(seqlen, (SSD_CHUNK, 64))
    assert row0 % q == 0 and pdim * 2 == LANES and hg % 2 == 0 and nstate == LANES
    nc = seqlen // q
    blk0 = row0 // q
    gw = hg * pdim
    row_map = lambda b, c: (blk0 + b * nc + c, 0)
    const = lambda b, c: (0, 0)
    seq4 = lambda b, c: (b, 0, 0, 0)
    return pl.pallas_call(
        functools.partial(_ssd_kernel, q=q, ngroups=ngroups, hg=hg, pdim=pdim, nstate=nstate),
        grid=(nseq, nc),
        in_specs=[pl.BlockSpec((q, dxbc), row_map),
                  pl.BlockSpec((q, d_inner), row_map),
                  pl.BlockSpec((q, LANES), row_map),
                  pl.BlockSpec((1, ngroups, nstate, gw), seq4),
                  pl.BlockSpec((1, LANES), const),
                  pl.BlockSpec((1, d_inner), const),
                  pl.BlockSpec((1, d_inner), const),
                  pl.BlockSpec((LANES, d_inner), const)],
        out_specs=[pl.BlockSpec((q, d_inner), lambda b, c: (b * nc + c, 0)),
                   pl.BlockSpec((1, ngroups, nstate, gw), seq4)],
        out_shape=[jax.ShapeDtypeStruct((nseq * seqlen, d_inner), BF16),
                   jax.ShapeDtypeStruct((nseq, ngroups, nstate, gw), F32)],
        scratch_shapes=[pltpu.VMEM((ngroups, q, gw), F32),
                        pltpu.VMEM((ngroups, nstate, gw), F32),
                        pltpu.VMEM((ngroups, hg, q), F32),
                        pltpu.VMEM((ngroups, hg, q), F32),
                        pltpu.VMEM((ngroups, hg, q), F32)],
        compiler_params=_cparams(("arbitrary", "arbitrary")),
        name=name,
    )(xbc, zs, dt, s0, a_row, dsk_row, ng_row, ex)


def _pad_hist(cache, hp):
    nseq, wm1, c = cache.shape
    return jnp.pad(cache.astype(F32), ((0, 0), (hp - wm1, 0), (0, 0)))


def _tail_rows(rows_all, row0, nseq, seqlen, keep):
    return jnp.stack([lax.slice_in_dim(rows_all, row0 + (b + 1) * seqlen - keep, row0 + (b + 1) * seqlen)
                      for b in range(nseq)]).astype(F32)


def _conv_tails(tail, keep):
    nseq, nw, _, c = tail.shape
    return tail[:, :, SUBLANES - keep:].transpose(0, 2, 1, 3).reshape(nseq, keep, nw * c)


def _layer(xp, xs, segs, cache_s, wts):
    (norm_mix_pre, w_in, b_gate, conv_a_w, conv_a_b, ln_a_g, ln_a_b, w_a_out, conv_b_w, conv_b_b,
     dt_bias, a_log, d_skip, ssd_norm_g, w_b_out, w_o, norm_mix_post, norm_ffn_pre, w_up,
     ffn_conv_w, ffn_conv_b, w_down, norm_ffn_post) = wts
    tp, d = xp.shape
    ts = xs.shape[0]
    (_, nseq_p, len_p), (_, nseq_s, len_s) = segs
    dc = conv_a_w.shape[1]
    wa = conv_a_w.shape[0]
    dxbc = conv_b_w.shape[1]
    wb = conv_b_w.shape[0]
    nheads = a_log.shape[0]
    d_inner = ssd_norm_g.shape[0]
    pdim = d_inner // nheads
    nstate = cache_s[2].shape[-1]
    ngroups = (dxbc - d_inner) // (2 * nstate)
    hg = nheads // ngroups
    gw = hg * pdim
    f2 = ffn_conv_w.shape[1]
    dff = f2 // 2
    wf = ffn_conv_w.shape[0]
    assert nheads <= LANES and min(len_p, len_s) >= max(wa, wb, wf) - 1

    row = lambda v: v.reshape(1, -1).astype(F32)
    tm = _pick(ts, (1024, 512, 256, 128, 64))
    assert tp % tm == 0 and len_p % tm == 0
    n_p = tp // tm
    seq_kw = dict(tm=tm, n_p=n_p, tiles_per_seq=len_p // tm, nseq_p=nseq_p, ls=len_s)

    h = _prenorm(xp, xs, row(norm_mix_pre), tm=tm)

    w_in_t = jnp.swapaxes(w_in, 0, 1)
    o_val, o_gate, o_z, o_xbc = 0, dc, 2 * dc, 2 * dc + d_inner
    o_dt = o_xbc + dxbc
    o_g = o_dt + nheads
    bh = _pick(dc, (512, 256, 128))
    u = _matmul(h, [(w_in_t, bh, o_val // bh), (w_in_t, bh, o_gate // bh)], nj=dc // bh, order="col", tm=tm,
                outs=[(dc, bh, BF16, False)], epilogue=_ep_glu, cast_w=True, w_t=True, name="inproj_glu")
    bz = _pick(d_inner, (1024, 512, 256, 128))
    assert o_z % bz == 0
    zs = _matmul(h, [(w_in_t, bz, o_z // bz)], nj=d_inner // bz, order="col", tm=tm,
                 outs=[(d_inner, bz, BF16, False)], epilogue=_ep_silu, cast_w=True, w_t=True, name="inproj_z")
    bx = _pick(dxbc, (1024, 512, 256, 128))
    assert o_xbc % bx == 0
    xbc_act, xbc_tail_p, xbc_tail_s = _proj_conv(
        h, w_in_t, [o_xbc // bx], conv_b_w.astype(F32), row(conv_b_b), _pad_hist(cache_s[1], SUBLANES), [0],
        nj=dxbc // bx, bw=bx, w_t=True, act_fn=_act_silu, name="inproj_xbc", **seq_kw)
    assert o_dt % LANES == 0 and o_dt + LANES <= w_in_t.shape[0]
    dt_b = jnp.pad(dt_bias.astype(F32), (0, LANES - nheads)).reshape(1, LANES)
    dt = _matmul(h, [(w_in_t, LANES, o_dt // LANES)], nj=1, order="col", tm=tm,
                 outs=[(LANES, LANES, F32, False)], epilogue=functools.partial(_ep_dt, nheads=nheads),
                 rows=[(dt_b, LANES, 0)], cast_w=True, w_t=True, name="inproj_dt")
    bg = _pick(2 * d, (1024, 512, 256, 128))
    gates = _matmul(h, [(w_in_t[o_g:].astype(BF16), bg, 0)], nj=2 * d // bg, order="col", tm=tm,
                    outs=[(2 * d, bg, BF16, False)], epilogue=_ep_bias_sigmoid, w_t=True,
                    rows=[(row(b_gate), bg, 0)], name="inproj_gates")

    hp_a = -(-(wa - 1) // SUBLANES) * SUBLANES
    hist_a = [jnp.zeros((nseq_p, hp_a, dc), F32), _pad_hist(cache_s[0], hp_a)]
    a_act = tuple(_conva_call(u, hist_a[si], conv_a_w.astype(F32), row(conv_a_b), row(ln_a_g), row(ln_a_b),
                              row0=row0, nseq=nseq, seqlen=seqlen, name=f"conva_{si}")
                  for si, (row0, nseq, seqlen) in enumerate(segs))
    bd = _pick(d, (1024, 512, 256, 128))
    ya = _matmul(a_act, [(w_a_out.astype(BF16), bd, 0)], nj=d // bd, order="row", tm=tm, n_p=n_p,
                 outs=[(d, bd, BF16, False)], epilogue=_ep_gate, blks=[(gates, bd, 0)], name="proj_a_out")

    a_row = jnp.pad(-jnp.exp(a_log.astype(F32)), (0, LANES - nheads)).reshape(1, LANES)
    dsk_row = jnp.repeat(d_skip.astype(F32), pdim).reshape(1, d_inner)
    ex = (jnp.arange(LANES)[:, None] == (jnp.arange(d_inner)[None, :] // pdim)).astype(BF16)
    s0 = [jnp.zeros((nseq_p, ngroups, nstate, gw), F32),
          cache_s[2].astype(F32).reshape(nseq_s, ngroups, gw, nstate).transpose(0, 1, 3, 2)]
    yn, new_states = [], []
    for si, (row0, nseq, seqlen) in enumerate(segs):
        y_si, s_new = _ssd_call(xbc_act, zs, dt, s0[si], a_row, dsk_row, row(ssd_norm_g), ex,
                                row0=row0, nseq=nseq, seqlen=seqlen, ngroups=ngroups, hg=hg,
                                pdim=pdim, nstate=nstate, name=f"ssd_{si}")
        yn.append(y_si)
        new_states.append(s_new.transpose(0, 1, 3, 2).reshape(nseq, nheads, pdim, nstate))
    tm_b = _pick(tm, (512, 256, 128, 64))
    merged = _matmul(tuple(yn), [(w_b_out.astype(BF16), bd, 0)], nj=d // bd, order="row", tm=tm_b,
                     n_p=tp // tm_b, outs=[(d, bd, BF16, False)], epilogue=_ep_gate_add,
                     blks=[(gates, bd, d // bd), (ya, bd, 0)], name="proj_b_out")
    tm_o = _pick(tm, (256, 128, 64))
    x1, h2 = _matmul(merged, [(w_o.astype(BF16), d, 0)], nj=1, order="row", tm=tm_o,
                     outs=[(d, d, F32, False), (d, d, BF16, False)], epilogue=_ep_residual_norm,
                     rows=[(row(norm_mix_post), d, 0), (row(norm_ffn_pre), d, 0)],
                     blks=[((xp, xs), d, 0)], n_p=tp // tm_o, single_buffer_w=True, name="proj_o")

    bf = _pick(dff, (512, 256, 128))
    act, ffn_tail_p, ffn_tail_s = _proj_conv(
        h2, w_up, [0, dff // bf], ffn_conv_w.astype(F32), row(ffn_conv_b), _pad_hist(cache_s[3], SUBLANES),
        [0, dff // bf], nj=dff // bf, bw=bf, w_t=False, act_fn=_act_gelu_gate, name="ffn_up", **seq_kw)
    yp, ys = _matmul(act, [(w_down.astype(BF16), d, 0)], nj=1, order="row", tm=tm_o,
                     outs=[(d, d, F32, True)], epilogue=_ep_residual,
                     rows=[(row(norm_ffn_post), d, 0)], blks=[(x1, d, 0)], n_p=tp // tm_o,
                     single_buffer_w=True, name="ffn_down")

    new_caches = []
    for si, (row0, nseq, seqlen) in enumerate(segs):
        new_caches.append((_tail_rows(u, row0, nseq, seqlen, wa - 1),
                           _conv_tails((xbc_tail_p, xbc_tail_s)[si], wb - 1),
                           new_states[si],
                           _conv_tails((ffn_tail_p, ffn_tail_s)[si], wf - 1)))
    return yp, ys, new_caches


def kernel(x_prompt, x_sample, cache_conv_a, cache_conv_b, state_ssd, cache_ffn_conv, norm_mix_pre, w_in, b_gate, conv_a_w, conv_a_b, ln_a_g, ln_a_b, w_a_out, conv_b_w, conv_b_b, dt_bias, a_log, d_skip, ssd_norm_g, w_b_out, w_o, norm_mix_post, norm_ffn_pre, w_up, ffn_conv_w, ffn_conv_b, w_down, norm_ffn_post):
    weights = (norm_mix_pre, w_in, b_gate, conv_a_w, conv_a_b, ln_a_g, ln_a_b, w_a_out, conv_b_w,
               conv_b_b, dt_bias, a_log, d_skip, ssd_norm_g, w_b_out, w_o, norm_mix_post,
               norm_ffn_pre, w_up, ffn_conv_w, ffn_conv_b, w_down, norm_ffn_post)
    depth = w_in.shape[0]
    bp, lp, d = x_prompt.shape
    bs, ls, _ = x_sample.shape
    tp = bp * lp
    segs = ((0, bp, lp), (tp, bs, ls))
    xp = x_prompt.reshape(tp, d)
    xs = x_sample.reshape(bs * ls, d)
    dt_in = x_prompt.dtype
    outs_p = ([], [], [], [])
    outs_s = ([], [], [], [])
    for layer in range(depth):
        wl = tuple(w[layer] for w in weights)
        cache_s = (cache_conv_a[layer], cache_conv_b[layer], state_ssd[layer], cache_ffn_conv[layer])
        xp, xs, (new_p, new_s) = _layer(xp, xs, segs, cache_s, wl)
        for lst, v in zip(outs_p, new_p):
            lst.append(v.astype(dt_in))
        for lst, v in zip(outs_s, new_s):
            lst.append(v.astype(dt_in))
    return (xp.reshape(bp, lp, d), xs.reshape(bs, ls, d),
            jnp.stack(outs_p[0]), jnp.stack(outs_p[1]), jnp.stack(outs_p[2]), jnp.stack(outs_p[3]),
            jnp.stack(outs_s[0]), jnp.stack(outs_s[1]), jnp.stack(outs_s[2]), jnp.stack(outs_s[3]))
```

```python
import functools

import jax
import jax.numpy as jnp
from jax import lax
from jax.experimental import pallas as pl
from jax.experimental.pallas import tpu as pltpu

F32 = jnp.float32
BF16 = jnp.bfloat16
EPS = 1e-6
LANES = 128
SUBLANES = 8
VMEM_LIMIT = 56 * 1024 * 1024
SSD_CHUNK = 128
CONV_LANES = 512
NEG_BIG = -1e30
NN_DIMS = (((1,), (0,)), ((), ()))
NT_DIMS = (((1,), (1,)), ((), ()))


def _pick(n, cands):
    for c in cands:
        if n % c == 0:
            return c
    raise ValueError(f"no tile in {cands} divides {n}")


def _cparams(sem):
    return pltpu.CompilerParams(dimension_semantics=sem, vmem_limit_bytes=VMEM_LIMIT)


def _sigmoid(x):
    return 1.0 / (1.0 + jnp.exp(-x))


def _silu(x):
    return x * _sigmoid(x)


def _softplus(x):
    return jnp.maximum(x, 0.0) + jnp.log1p(jnp.exp(-jnp.abs(x)))


def _rms_scale(x, eps):
    return x * lax.rsqrt(jnp.mean(x * x, axis=-1, keepdims=True) + eps)


def _prenorm_kernel(xp_ref, xs_ref, g_ref, o_ref, *, n_p):
    i = pl.program_id(0)

    def run(x_ref):
        o_ref[...] = (_rms_scale(x_ref[...].astype(F32), EPS) * g_ref[...]).astype(o_ref.dtype)

    pl.when(i < n_p)(functools.partial(run, xp_ref))
    pl.when(i >= n_p)(functools.partial(run, xs_ref))


def _prenorm(xp, xs, gain, *, tm):
    tp, d = xp.shape
    ts = xs.shape[0]
    n_p = tp // tm
    return pl.pallas_call(
        functools.partial(_prenorm_kernel, n_p=n_p),
        grid=((tp + ts) // tm,),
        in_specs=[pl.BlockSpec((tm, d), lambda i: (jnp.minimum(i, n_p - 1), 0)),
                  pl.BlockSpec((tm, d), lambda i: (jnp.maximum(i - n_p, 0), 0)),
                  pl.BlockSpec((1, d), lambda i: (0, 0))],
        out_specs=pl.BlockSpec((tm, d), lambda i: (i, 0)),
        out_shape=jax.ShapeDtypeStruct((tp + ts, d), BF16),
        compiler_params=_cparams(("arbitrary",)),
        name="prenorm",
    )(xp, xs, gain)


def _mm_kernel(*refs, order, n_w, cast_w, w_t, lhs_dual, n_rows, blk_dual, out_dual, n_p, epilogue):
    it = iter(refs)
    lhs_refs = tuple(next(it) for _ in range(2 if lhs_dual else 1))
    w_refs = [next(it) for _ in range(n_w)]
    row_refs = [next(it) for _ in range(n_rows)]
    blk_refs = [tuple(next(it) for _ in range(2 if d else 1)) for d in blk_dual]
    out_refs = [tuple(next(it) for _ in range(2 if d else 1)) for d in out_dual]
    i = pl.program_id(1 if order == "col" else 0)
    if cast_w:
        wbuf = next(it)

        @pl.when(i == 0)
        def _():
            c0 = 0
            for wr in w_refs:
                if w_t:
                    wbuf[c0:c0 + wr.shape[0], :] = wr[...].astype(BF16)
                    c0 += wr.shape[0]
                else:
                    wbuf[:, c0:c0 + wr.shape[1]] = wr[...].astype(BF16)
                    c0 += wr.shape[1]

        w_ref = wbuf
    else:
        w_ref = w_refs[0]

    def run(sel):
        pick = lambda pair: pair[sel if len(pair) > 1 else 0]
        r = lax.dot_general(pick(lhs_refs)[...], w_ref[...], NT_DIMS if w_t else NN_DIMS,
                            preferred_element_type=F32)
        epilogue(r, row_refs, [pick(b) for b in blk_refs], [pick(o) for o in out_refs])

    if lhs_dual or any(blk_dual) or any(out_dual):
        pl.when(i < n_p)(functools.partial(run, 0))
        pl.when(i >= n_p)(functools.partial(run, 1))
    else:
        run(0)


def _matmul(lhs, ws, *, nj, order, tm, outs, epilogue, name, rows=(), blks=(), cast_w=False,
            w_t=False, n_p=None, single_buffer_w=False):
    lhs_dual = isinstance(lhs, tuple)
    t = lhs[0].shape[0] + lhs[1].shape[0] if lhs_dual else lhs.shape[0]
    k = lhs[0].shape[1] if lhs_dual else lhs.shape[1]
    ni = t // tm
    if order == "col":
        grid = (nj, ni)
        spec = lambda shape, fn, **kw: pl.BlockSpec(shape, lambda a, b: fn(b, a), **kw)
    else:
        grid = (ni, nj)
        spec = lambda shape, fn, **kw: pl.BlockSpec(shape, lambda a, b: fn(a, b), **kw)
    assert not cast_w or order == "col" or nj == 1
    prompt_row = lambda i: jnp.minimum(i, n_p - 1)
    decode_row = lambda i: jnp.maximum(i - n_p, 0)

    if lhs_dual:
        in_specs = [spec((tm, k), lambda i, j: (prompt_row(i), 0)),
                    spec((tm, k), lambda i, j: (decode_row(i), 0))]
        args = list(lhs)
    else:
        in_specs = [spec((tm, k), lambda i, j: (i, 0))]
        args = [lhs]
    w_kw = {"pipeline_mode": pl.Buffered(1)} if single_buffer_w else {}
    bn = 0
    for arr, bw, off in ws:
        if w_t:
            in_specs.append(spec((bw, k), lambda i, j, off=off: (j + off, 0), **w_kw))
        else:
            in_specs.append(spec((k, bw), lambda i, j, off=off: (0, j + off), **w_kw))
        args.append(arr)
        bn += bw
    for arr, bw, off in rows:
        in_specs.append(spec((1, bw), lambda i, j, off=off: (0, j + off)))
        args.append(arr)
    for arr, bw, off in blks:
        if isinstance(arr, tuple):
            in_specs.append(spec((tm, bw), lambda i, j, off=off: (prompt_row(i), j + off)))
            in_specs.append(spec((tm, bw), lambda i, j, off=off: (decode_row(i), j + off)))
            args.extend(arr)
        else:
            in_specs.append(spec((tm, bw), lambda i, j, off=off: (i, j + off)))
            args.append(arr)
    out_specs, out_shape = [], []
    for cols, bw, dtype, dual in outs:
        if dual:
            out_specs.append(spec((tm, bw), lambda i, j: (prompt_row(i), j)))
            out_specs.append(spec((tm, bw), lambda i, j: (decode_row(i), j)))
            out_shape.append(jax.ShapeDtypeStruct((n_p * tm, cols), dtype))
            out_shape.append(jax.ShapeDtypeStruct(((ni - n_p) * tm, cols), dtype))
        else:
            out_specs.append(spec((tm, bw), lambda i, j: (i, j)))
            out_shape.append(jax.ShapeDtypeStruct((t, cols), dtype))
    scratch = [pltpu.VMEM((bn, k) if w_t else (k, bn), BF16)] if cast_w else []
    res = pl.pallas_call(
        functools.partial(_mm_kernel, order=order, n_w=len(ws), cast_w=cast_w, w_t=w_t,
                          lhs_dual=lhs_dual, n_rows=len(rows),
                          blk_dual=[isinstance(b[0], tuple) for b in blks],
                          out_dual=[o[3] for o in outs], n_p=n_p, epilogue=epilogue),
        grid=grid,
        in_specs=in_specs,
        out_specs=out_specs,
        out_shape=out_shape,
        scratch_shapes=scratch,
        compiler_params=_cparams(("arbitrary", "arbitrary")),
        name=name,
    )(*args)
    return res[0] if len(res) == 1 else res


def _ep_glu(r, rows, blks, outs):
    half = r.shape[1] // 2
    outs[0][...] = (r[:, :half] * _sigmoid(r[:, half:])).astype(outs[0].dtype)


def _ep_silu(r, rows, blks, outs):
    outs[0][...] = _silu(r).astype(outs[0].dtype)


def _ep_store(r, rows, blks, outs):
    outs[0][...] = r.astype(outs[0].dtype)


def _ep_bias_sigmoid(r, rows, blks, outs):
    outs[0][...] = _sigmoid(r + rows[0][...]).astype(outs[0].dtype)


def _ep_dt(r, rows, blks, outs, *, nheads):
    lane = lax.broadcasted_iota(jnp.int32, r.shape, 1)
    outs[0][...] = jnp.where(lane < nheads, _softplus(r + rows[0][...]), 0.0).astype(outs[0].dtype)


def _ep_gate(r, rows, blks, outs):
    outs[0][...] = (r * blks[0][...].astype(F32)).astype(outs[0].dtype)


def _ep_gate_add(r, rows, blks, outs):
    outs[0][...] = (r * blks[0][...].astype(F32) + blks[1][...].astype(F32)).astype(outs[0].dtype)


def _ep_residual_norm(r, rows, blks, outs):
    x1 = blks[0][...] + _rms_scale(r, EPS) * rows[0][...]
    outs[0][...] = x1
    outs[1][...] = (_rms_scale(x1, EPS) * rows[1][...]).astype(outs[1].dtype)


def _ep_residual(r, rows, blks, outs):
    outs[0][...] = blks[0][...] + _rms_scale(r, EPS) * rows[0][...]


def _ffnup_kernel(h_ref, wg_ref, wv_ref, cwg_ref, cwv_ref, cbg_ref, cbv_ref, hg_ref, hv_ref,
                  act_ref, tailp_ref, tails_ref, wbuf, cbuf, sbuf,
                  *, width, tm, bw, sm, n_p, tiles_per_seq, ls):
    hp = SUBLANES
    i = pl.program_id(1)

    @pl.when(i == 0)
    def _():
        wbuf[:, 0:bw] = wg_ref[...].astype(BF16)
        wbuf[:, bw:2 * bw] = wv_ref[...].astype(BF16)

    w = wbuf[...]

    def conv_act(buf, r0, n):
        def conv(c0, cw_ref, cb_ref):
            acc = jnp.zeros((n, bw), F32) + cb_ref[...]
            for k in range(width):
                off = r0 - (width - 1) + k
                acc = acc + buf[off:off + n, c0:c0 + bw] * cw_ref[k:k + 1, :]
            return acc
        return (jax.nn.gelu(conv(0, cwg_ref, cbg_ref)) * conv(bw, cwv_ref, cbv_ref)).astype(act_ref.dtype)

    @pl.when(i < n_p)
    def _():
        @pl.when(i % tiles_per_seq == 0)
        def _():
            cbuf[0:hp, :] = jnp.zeros((hp, 2 * bw), F32)

        for s in range(tm // sm):
            cbuf[hp + s * sm:hp + (s + 1) * sm, :] = jnp.dot(
                h_ref[s * sm:(s + 1) * sm, :], w, preferred_element_type=F32)
            act_ref[s * sm:(s + 1) * sm, :] = conv_act(cbuf, hp + s * sm, sm)

        @pl.when(i % tiles_per_seq == tiles_per_seq - 1)
        def _():
            tailp_ref[0, 0] = cbuf[tm:tm + hp, 0:bw]
            tailp_ref[0, 1] = cbuf[tm:tm + hp, bw:2 * bw]

        cbuf[0:hp, :] = cbuf[tm:tm + hp, :]

    @pl.when(i >= n_p)
    def _():
        for s in range(tm // sm):
            r = jnp.dot(h_ref[s * sm:(s + 1) * sm, :], w, preferred_element_type=F32)
            for q in range(sm // ls):
                seq = s * (sm // ls) + q
                sbuf[0:hp, 0:bw] = hg_ref[seq]
                sbuf[0:hp, bw:2 * bw] = hv_ref[seq]
                rq = r[q * ls:(q + 1) * ls, :]
                sbuf[hp:hp + ls, :] = rq
                act_ref[s * sm + q * ls:s * sm + (q + 1) * ls, :] = conv_act(sbuf, hp, ls)
                tails_ref[seq, 0] = rq[ls - hp:ls, 0:bw]
                tails_ref[seq, 1] = rq[ls - hp:ls, bw:2 * bw]


def _ffn_up_conv(h, w_up, conv_w, conv_b, hist_s, *, tm, n_p, tiles_per_seq, nseq_p, ls, name):
    t, k = h.shape
    f2 = w_up.shape[1]
    dff = f2 // 2
    width = conv_w.shape[0]
    bw = _pick(dff, (512, 256, 128))
    nj = dff // bw
    ni = t // tm
    sm = _pick(tm, (256, 128, 64))
    spt = tm // ls
    nseq_s = hist_s.shape[0]
    assert tm % ls == 0 and sm % ls == 0 and ls >= SUBLANES and width - 1 <= SUBLANES
    decode_tile = lambda i: jnp.maximum(i - n_p, 0)
    prompt_seq = lambda i: jnp.minimum(i, n_p - 1) // tiles_per_seq
    in_specs = [pl.BlockSpec((tm, k), lambda j, i: (i, 0)),
                pl.BlockSpec((k, bw), lambda j, i: (0, j)),
                pl.BlockSpec((k, bw), lambda j, i: (0, nj + j)),
                pl.BlockSpec((width, bw), lambda j, i: (0, j)),
                pl.BlockSpec((width, bw), lambda j, i: (0, nj + j)),
                pl.BlockSpec((1, bw), lambda j, i: (0, j)),
                pl.BlockSpec((1, bw), lambda j, i: (0, nj + j)),
                pl.BlockSpec((spt, SUBLANES, bw), lambda j, i: (decode_tile(i), 0, j)),
                pl.BlockSpec((spt, SUBLANES, bw), lambda j, i: (decode_tile(i), 0, nj + j))]
    out_specs = [pl.BlockSpec((tm, bw), lambda j, i: (i, j)),
                 pl.BlockSpec((1, 2, SUBLANES, bw), lambda j, i: (prompt_seq(i), 0, 0, j)),
                 pl.BlockSpec((spt, 2, SUBLANES, bw), lambda j, i: (decode_tile(i), 0, 0, j))]
    out_shape = [jax.ShapeDtypeStruct((t, dff), BF16),
                 jax.ShapeDtypeStruct((nseq_p, 2, SUBLANES, dff), F32),
                 jax.ShapeDtypeStruct((nseq_s, 2, SUBLANES, dff), F32)]
    return pl.pallas_call(
        functools.partial(_ffnup_kernel, width=width, tm=tm, bw=bw, sm=sm, n_p=n_p,
                          tiles_per_seq=tiles_per_seq, ls=ls),
        grid=(nj, ni),
        in_specs=in_specs,
        out_specs=out_specs,
        out_shape=out_shape,
        scratch_shapes=[pltpu.VMEM((k, 2 * bw), BF16),
                        pltpu.VMEM((SUBLANES + tm, 2 * bw), F32),
                        pltpu.VMEM((SUBLANES + ls, 2 * bw), F32)],
        compiler_params=_cparams(("arbitrary", "arbitrary")),
        name=name,
    )(h, w_up, w_up, conv_w, conv_w, conv_b, conv_b, hist_s, hist_s)


def _conva_kernel(u_ref, hist_ref, w_ref, cb_ref, lg_ref, lb_ref, o_ref, bufp_ref, cv_ref,
                  *, width, hp, ta, rc, lc):
    c = u_ref.shape[1]
    t = pl.program_id(1)
    n_ext = hp + ta

    @pl.when(t == 0)
    def _():
        bufp_ref[0, 0:hp, :] = hist_ref[0]
        bufp_ref[0, n_ext:n_ext + SUBLANES, :] = jnp.zeros((SUBLANES, c), F32)

    bufp_ref[0, hp:hp + ta, :] = u_ref[...].astype(F32)

    def shift_body(r, carry):
        s0 = pl.multiple_of(r * rc, rc)
        win = bufp_ref[0, pl.ds(s0, rc + SUBLANES), :]
        for p in range(1, SUBLANES):
            bufp_ref[p, pl.ds(s0, rc), :] = win[p:p + rc, :]
        return carry

    lax.fori_loop(0, n_ext // rc, shift_body, 0)

    def conv_body(r, carry):
        r0 = pl.multiple_of(r * rc, rc)
        for l0 in range(0, c, lc):
            acc = jnp.zeros((rc, lc), F32)
            for k in range(width):
                off = hp - (width - 1) + k
                q8, p = (off // SUBLANES) * SUBLANES, off % SUBLANES
                xs = bufp_ref[p, pl.ds(r0 + q8, rc), l0:l0 + lc]
                acc = acc + xs * w_ref[k:k + 1, l0:l0 + lc]
            cv_ref[pl.ds(r0, rc), l0:l0 + lc] = acc
        return carry

    lax.fori_loop(0, ta // rc, conv_body, 0)
    bufp_ref[0, 0:hp, :] = bufp_ref[0, ta:ta + hp, :]

    v = cv_ref[...] + cb_ref[...]
    mu = jnp.mean(v, axis=-1, keepdims=True)
    vc = v - mu
    var = jnp.mean(vc * vc, axis=-1, keepdims=True)
    y = vc * lax.rsqrt(var + EPS) * lg_ref[...] + lb_ref[...]
    o_ref[...] = _silu(y).astype(o_ref.dtype)


def _conva_call(u, hist, w, cb, lg, lb, *, row0, nseq, seqlen, name):
    c = u.shape[1]
    width = w.shape[0]
    hp = hist.shape[1]
    ta = _pick(seqlen, (256, 128, 64))
    rc = 32
    lc = 256
    assert row0 % ta == 0 and (hp + ta) % rc == 0 and hp >= width - 1
    nt = seqlen // ta
    blk0 = row0 // ta
    const = lambda b, t: (0, 0)
    return pl.pallas_call(
        functools.partial(_conva_kernel, width=width, hp=hp, ta=ta, rc=rc, lc=lc),
        grid=(nseq, nt),
        in_specs=[pl.BlockSpec((ta, c), lambda b, t: (blk0 + b * nt + t, 0)),
                  pl.BlockSpec((1, hp, c), lambda b, t: (b, 0, 0)),
                  pl.BlockSpec((width, c), const),
                  pl.BlockSpec((1, c), const),
                  pl.BlockSpec((1, c), const),
                  pl.BlockSpec((1, c), const)],
        out_specs=pl.BlockSpec((ta, c), lambda b, t: (b * nt + t, 0)),
        out_shape=jax.ShapeDtypeStruct((nseq * seqlen, c), BF16),
        scratch_shapes=[pltpu.VMEM((SUBLANES, hp + ta + SUBLANES, c), F32), pltpu.VMEM((ta, c), F32)],
        compiler_params=_cparams(("arbitrary", "arbitrary")),
        name=name,
    )(u, hist, w, cb, lg, lb)


def _ssd_kernel(raw_ref, zs_ref, dt_ref, hist_ref, s0_ref, cw_ref, cb_ref, a_ref, dsk_ref, ng_ref, ex_ref,
                y_ref, sout_ref, xbc_ref, hbuf_ref, y3_ref, st_ref, cumt_ref, cml_ref, wgt_ref,
                *, width, q, ngroups, hg, pdim, nstate):
    gw = hg * pdim
    d_inner = ngroups * gw
    c = pl.program_id(1)
    nchunks = pl.num_programs(1)

    @pl.when(c == 0)
    def _():
        st_ref[...] = s0_ref[0]
        hbuf_ref[...] = hist_ref[0]

    for l0 in range(0, raw_ref.shape[1], CONV_LANES):
        cols = slice(l0, l0 + CONV_LANES)
        raw = raw_ref[:, cols].astype(F32)
        e = jnp.concatenate([hbuf_ref[:, cols], raw], axis=0)
        acc = e * cw_ref[0:1, cols]
        for k in range(1, width):
            acc = e * cw_ref[k:k + 1, cols] + pltpu.roll(acc, 1, axis=0)
        v = acc[SUBLANES:, :] + cb_ref[:, cols]
        xbc_ref[:, cols] = (0.5 * v * (1.0 + jnp.tanh(0.5 * v))).astype(xbc_ref.dtype)
        hbuf_ref[:, cols] = raw[q - SUBLANES:q, :]

    dt = dt_ref[...]
    da = dt * a_ref[...]
    ii = lax.broadcasted_iota(jnp.int32, (q, q), 0)
    jj = lax.broadcasted_iota(jnp.int32, (q, q), 1)
    causal = ii >= jj
    tri = jnp.where(causal, 1.0, 0.0).astype(BF16)
    da_hi = da.astype(BF16)
    da_lo = (da - da_hi.astype(F32)).astype(BF16)
    cum = (jnp.dot(tri, da_hi, preferred_element_type=F32)
           + jnp.dot(tri, da_lo, preferred_element_type=F32))
    cum_t = cum.T
    dt_t = dt.T
    last_t = cum_t[:, q - 1:q]
    nh = ngroups * hg
    cumt_ref[...] = cum_t[0:nh, :].reshape(ngroups, hg, q)
    cml_ref[...] = (cum_t - jnp.log(dt_t))[0:nh, :].reshape(ngroups, hg, q)
    wgt_ref[...] = (dt_t * jnp.exp(last_t - cum_t))[0:nh, :].reshape(ngroups, hg, q)

    e_last = jnp.exp(cum[q - SUBLANES:q, :])
    e_hi = e_last.astype(BF16)
    e_lo = (e_last - e_hi.astype(F32)).astype(BF16)
    dec = (jnp.dot(e_hi, ex_ref[...], preferred_element_type=F32)
           + jnp.dot(e_lo, ex_ref[...], preferred_element_type=F32))
    dec_row = dec[SUBLANES - 1:SUBLANES, :]

    lane = lax.broadcasted_iota(jnp.int32, (1, LANES), 1)
    lo_mask = lane < pdim
    b0 = d_inner
    c0 = d_inner + ngroups * nstate

    for g in range(ngroups):
        bg = xbc_ref[:, b0 + g * nstate:b0 + (g + 1) * nstate]
        cg = xbc_ref[:, c0 + g * nstate:c0 + (g + 1) * nstate]
        scores = lax.dot_general(cg, bg, NT_DIMS, preferred_element_type=F32)
        bg_t = bg.astype(F32).T
        cg32 = cg.astype(F32)
        cum_g = cumt_ref[g]
        cml_g = cml_ref[g]
        wg_g = wgt_ref[g]
        for hp2 in range(hg // 2):
            l0 = g * gw + hp2 * LANES
            xpair = xbc_ref[:, l0:l0 + LANES]
            spair = st_ref[g, :, hp2 * LANES:(hp2 + 1) * LANES]
            lhs_parts, rhs_parts, bw_parts, x_parts = [], [], [], []
            for part in range(2):
                hh = 2 * hp2 + part
                mask = lo_mask if part == 0 else jnp.logical_not(lo_mask)
                x_h = jnp.where(mask, xpair, jnp.zeros_like(xpair))
                s_h = jnp.where(mask, spair, 0.0).astype(BF16)
                row = cum_g[hh:hh + 1, :]
                col = jnp.broadcast_to(row, (LANES, q)).T
                seg = col[:, 0:q] - cml_g[hh:hh + 1, :]
                m_h = scores * jnp.exp(jnp.where(causal, seg, NEG_BIG))
                cs_h = cg32 * jnp.exp(col[:, 0:nstate])
                lhs_parts += [m_h.astype(BF16), cs_h.astype(BF16)]
                rhs_parts += [x_h, s_h]
                bw_parts.append((bg_t * wg_g[hh:hh + 1, :]).astype(BF16))
                x_parts.append(x_h)
            ypair = jnp.dot(jnp.concatenate(lhs_parts, axis=1), jnp.concatenate(rhs_parts, axis=0),
                            preferred_element_type=F32)
            upd = jnp.dot(jnp.concatenate(bw_parts, axis=1), jnp.concatenate(x_parts, axis=0),
                          preferred_element_type=F32)
            y3_ref[g, :, hp2 * LANES:(hp2 + 1) * LANES] = ypair
            st_ref[g, :, hp2 * LANES:(hp2 + 1) * LANES] = spair * dec_row[:, l0:l0 + LANES] + upd

    ssq = jnp.zeros((q, 1), F32)
    for g in range(ngroups):
        cols = slice(g * gw, (g + 1) * gw)
        yg = (y3_ref[g] + dsk_ref[:, cols] * xbc_ref[:, cols].astype(F32)) * zs_ref[:, cols].astype(F32)
        y3_ref[g] = yg
        ssq = ssq + jnp.sum(yg * yg, axis=-1, keepdims=True)
    inv = lax.rsqrt(ssq / d_inner + EPS)
    for g in range(ngroups):
        cols = slice(g * gw, (g + 1) * gw)
        y_ref[:, cols] = (y3_ref[g] * inv * ng_ref[:, cols]).astype(y_ref.dtype)

    @pl.when(c == nchunks - 1)
    def _():
        sout_ref[0] = st_ref[...]


def _ssd_call(xbc, zs, dt, hist, s0, conv_w, conv_b, a_row, dsk_row, ng_row, ex, *,
              row0, nseq, seqlen, ngroups, hg, pdim, nstate, name):
    dxbc = xbc.shape[1]
    width = conv_w.shape[0]
    assert dxbc % CONV_LANES == 0 and width - 1 <= SUBLANES
    d_inner = zs.shape[1]
    q = _pick(seqlen, (SSD_CHUNK, 64))
    assert row0 % q == 0 and pdim * 2 == LANES and hg % 2 == 0 and nstate == LANES
    nc = seqlen // q
    blk0 = row0 // q
    gw = hg * pdim
    row_map = lambda b, c: (blk0 + b * nc + c, 0)
    const = lambda b, c: (0, 0)
    seq4 = lambda b, c: (b, 0, 0, 0)
    return pl.pallas_call(
        functools.partial(_ssd_kernel, width=width, q=q, ngroups=ngroups, hg=hg, pdim=pdim, nstate=nstate),
        grid=(nseq, nc),
        in_specs=[pl.BlockSpec((q, dxbc), row_map),
                  pl.BlockSpec((q, d_inner), row_map),
                  pl.BlockSpec((q, LANES), row_map),
                  pl.BlockSpec((1, SUBLANES, dxbc), lambda b, c: (b, 0, 0)),
                  pl.BlockSpec((1, ngroups, nstate, gw), seq4),
                  pl.BlockSpec((width, dxbc), const),
                  pl.BlockSpec((1, dxbc), const),
                  pl.BlockSpec((1, LANES), const),
                  pl.BlockSpec((1, d_inner), const),
                  pl.BlockSpec((1, d_inner), const),
                  pl.BlockSpec((LANES, d_inner), const)],
        out_specs=[pl.BlockSpec((q, d_inner), lambda b, c: (b * nc + c, 0)),
                   pl.BlockSpec((1, ngroups, nstate, gw), seq4)],
        out_shape=[jax.ShapeDtypeStruct((nseq * seqlen, d_inner), BF16),
                   jax.ShapeDtypeStruct((nseq, ngroups, nstate, gw), F32)],
        scratch_shapes=[pltpu.VMEM((q, dxbc), BF16),
                        pltpu.VMEM((SUBLANES, dxbc), F32),
                        pltpu.VMEM((ngroups, q, gw), F32),
                        pltpu.VMEM((ngroups, nstate, gw), F32),
                        pltpu.VMEM((ngroups, hg, q), F32),
                        pltpu.VMEM((ngroups, hg, q), F32),
                        pltpu.VMEM((ngroups, hg, q), F32)],
        compiler_params=_cparams(("arbitrary", "arbitrary")),
        name=name,
    )(xbc, zs, dt, hist, s0, conv_w, conv_b, a_row, dsk_row, ng_row, ex)


def _pad_hist(cache, hp):
    nseq, wm1, c = cache.shape
    return jnp.pad(cache.astype(F32), ((0, 0), (hp - wm1, 0), (0, 0)))


def _tail_rows(rows_all, row0, nseq, seqlen, keep):
    return jnp.stack([lax.slice_in_dim(rows_all, row0 + (b + 1) * seqlen - keep, row0 + (b + 1) * seqlen)
                      for b in range(nseq)]).astype(F32)


def _conv_tails(tail, keep):
    nseq, nw, _, c = tail.shape
    return tail[:, :, SUBLANES - keep:].transpose(0, 2, 1, 3).reshape(nseq, keep, nw * c)


def _layer(xp, xs, segs, cache_s, wts):
    (norm_mix_pre, w_in, b_gate, conv_a_w, conv_a_b, ln_a_g, ln_a_b, w_a_out, conv_b_w, conv_b_b,
     dt_bias, a_log, d_skip, ssd_norm_g, w_b_out, w_o, norm_mix_post, norm_ffn_pre, w_up,
     ffn_conv_w, ffn_conv_b, w_down, norm_ffn_post) = wts
    tp, d = xp.shape
    ts = xs.shape[0]
    (_, nseq_p, len_p), (_, nseq_s, len_s) = segs
    dc = conv_a_w.shape[1]
    wa = conv_a_w.shape[0]
    dxbc = conv_b_w.shape[1]
    wb = conv_b_w.shape[0]
    nheads = a_log.shape[0]
    d_inner = ssd_norm_g.shape[0]
    pdim = d_inner // nheads
    nstate = cache_s[2].shape[-1]
    ngroups = (dxbc - d_inner) // (2 * nstate)
    hg = nheads // ngroups
    gw = hg * pdim
    f2 = ffn_conv_w.shape[1]
    dff = f2 // 2
    wf = ffn_conv_w.shape[0]
    assert nheads <= LANES and min(len_p, len_s) >= max(wa, wb, wf) - 1

    row = lambda v: v.reshape(1, -1).astype(F32)
    tm = _pick(ts, (1024, 512, 256, 128, 64))
    assert tp % tm == 0 and len_p % tm == 0
    n_p = tp // tm
    seq_kw = dict(tm=tm, n_p=n_p, tiles_per_seq=len_p // tm, nseq_p=nseq_p, ls=len_s)

    h = _prenorm(xp, xs, row(norm_mix_pre), tm=tm)

    w_in_t = jnp.swapaxes(w_in, 0, 1)
    o_val, o_gate, o_z, o_xbc = 0, dc, 2 * dc, 2 * dc + d_inner
    o_dt = o_xbc + dxbc
    o_g = o_dt + nheads
    bh = _pick(dc, (512, 256, 128))
    u = _matmul(h, [(w_in_t, bh, o_val // bh), (w_in_t, bh, o_gate // bh)], nj=dc // bh, order="col", tm=tm,
                outs=[(dc, bh, BF16, False)], epilogue=_ep_glu, cast_w=True, w_t=True, name="inproj_glu")
    bz = _pick(d_inner, (1024, 512, 256, 128))
    assert o_z % bz == 0
    zs = _matmul(h, [(w_in_t, bz, o_z // bz)], nj=d_inner // bz, order="col", tm=tm,
                 outs=[(d_inner, bz, BF16, False)], epilogue=_ep_silu, cast_w=True, w_t=True, name="inproj_z")
    bx = _pick(dxbc, (1024, 512, 256, 128))
    assert o_xbc % bx == 0
    xbc = _matmul(h, [(w_in_t, bx, o_xbc // bx)], nj=dxbc // bx, order="col", tm=tm,
                  outs=[(dxbc, bx, BF16, False)], epilogue=_ep_store, cast_w=True, w_t=True, name="inproj_xbc")
    assert o_dt % LANES == 0 and o_dt + LANES <= w_in_t.shape[0]
    dt_b = jnp.pad(dt_bias.astype(F32), (0, LANES - nheads)).reshape(1, LANES)
    dt = _matmul(h, [(w_in_t, LANES, o_dt // LANES)], nj=1, order="col", tm=tm,
                 outs=[(LANES, LANES, F32, False)], epilogue=functools.partial(_ep_dt, nheads=nheads),
                 rows=[(dt_b, LANES, 0)], cast_w=True, w_t=True, name="inproj_dt")
    bg = _pick(2 * d, (1024, 512, 256, 128))
    gates = _matmul(h, [(w_in_t[o_g:].astype(BF16), bg, 0)], nj=2 * d // bg, order="col", tm=tm,
                    outs=[(2 * d, bg, BF16, False)], epilogue=_ep_bias_sigmoid, w_t=True,
                    rows=[(row(b_gate), bg, 0)], name="inproj_gates")

    hp_a = -(-(wa - 1) // SUBLANES) * SUBLANES
    hist_a = [jnp.zeros((nseq_p, hp_a, dc), F32), _pad_hist(cache_s[0], hp_a)]
    a_act = tuple(_conva_call(u, hist_a[si], conv_a_w.astype(F32), row(conv_a_b), row(ln_a_g), row(ln_a_b),
                              row0=row0, nseq=nseq, seqlen=seqlen, name=f"conva_{si}")
                  for si, (row0, nseq, seqlen) in enumerate(segs))
    bd = _pick(d, (1024, 512, 256, 128))
    ya = _matmul(a_act, [(w_a_out.astype(BF16), bd, 0)], nj=d // bd, order="row", tm=tm, n_p=n_p,
                 outs=[(d, bd, BF16, False)], epilogue=_ep_gate, blks=[(gates, bd, 0)], name="proj_a_out")

    a_row = jnp.pad(-jnp.exp(a_log.astype(F32)), (0, LANES - nheads)).reshape(1, LANES)
    dsk_row = jnp.repeat(d_skip.astype(F32), pdim).reshape(1, d_inner)
    ex = (jnp.arange(LANES)[:, None] == (jnp.arange(d_inner)[None, :] // pdim)).astype(BF16)
    s0 = [jnp.zeros((nseq_p, ngroups, nstate, gw), F32),
          cache_s[2].astype(F32).reshape(nseq_s, ngroups, gw, nstate).transpose(0, 1, 3, 2)]
    hist_b = [jnp.zeros((nseq_p, SUBLANES, dxbc), F32), _pad_hist(cache_s[1], SUBLANES)]
    yn, new_states = [], []
    for si, (row0, nseq, seqlen) in enumerate(segs):
        y_si, s_new = _ssd_call(xbc, zs, dt, hist_b[si], s0[si], conv_b_w.astype(F32), row(conv_b_b),
                                a_row, dsk_row, row(ssd_norm_g), ex,
                                row0=row0, nseq=nseq, seqlen=seqlen, ngroups=ngroups, hg=hg,
                                pdim=pdim, nstate=nstate, name=f"ssd_{si}")
        yn.append(y_si)
        new_states.append(s_new.transpose(0, 1, 3, 2).reshape(nseq, nheads, pdim, nstate))
    tm_b = _pick(tm, (512, 256, 128, 64))
    merged = _matmul(tuple(yn), [(w_b_out.astype(BF16), bd, 0)], nj=d // bd, order="row", tm=tm_b,
                     n_p=tp // tm_b, outs=[(d, bd, BF16, False)], epilogue=_ep_gate_add,
                     blks=[(gates, bd, d // bd), (ya, bd, 0)], name="proj_b_out")
    tm_o = _pick(tm, (256, 128, 64))
    x1, h2 = _matmul(merged, [(w_o.astype(BF16), d, 0)], nj=1, order="row", tm=tm_o,
                     outs=[(d, d, F32, False), (d, d, BF16, False)], epilogue=_ep_residual_norm,
                     rows=[(row(norm_mix_post), d, 0), (row(norm_ffn_pre), d, 0)],
                     blks=[((xp, xs), d, 0)], n_p=tp // tm_o, single_buffer_w=True, name="proj_o")

    act, ffn_tail_p, ffn_tail_s = _ffn_up_conv(h2, w_up, ffn_conv_w.astype(F32), row(ffn_conv_b),
                                               _pad_hist(cache_s[3], SUBLANES), name="ffn_up", **seq_kw)
    yp, ys = _matmul(act, [(w_down.astype(BF16), d, 0)], nj=1, order="row", tm=tm_o,
                     outs=[(d, d, F32, True)], epilogue=_ep_residual,
                     rows=[(row(norm_ffn_post), d, 0)], blks=[(x1, d, 0)], n_p=tp // tm_o,
                     single_buffer_w=True, name="ffn_down")

    new_caches = []
    for si, (row0, nseq, seqlen) in enumerate(segs):
        new_caches.append((_tail_rows(u, row0, nseq, seqlen, wa - 1),
                           _tail_rows(xbc, row0, nseq, seqlen, wb - 1),
                           new_states[si],
                           _conv_tails((ffn_tail_p, ffn_tail_s)[si], wf - 1)))
    return yp, ys, new_caches


def kernel(x_prompt, x_sample, cache_conv_a, cache_conv_b, state_ssd, cache_ffn_conv, norm_mix_pre, w_in, b_gate, conv_a_w, conv_a_b, ln_a_g, ln_a_b, w_a_out, conv_b_w, conv_b_b, dt_bias, a_log, d_skip, ssd_norm_g, w_b_out, w_o, norm_mix_post, norm_ffn_pre, w_up, ffn_conv_w, ffn_conv_b, w_down, norm_ffn_post):
    weights = (norm_mix_pre, w_in, b_gate, conv_a_w, conv_a_b, ln_a_g, ln_a_b, w_a_out, conv_b_w,
               conv_b_b, dt_bias, a_log, d_skip, ssd_norm_g, w_b_out, w_o, norm_mix_post,
               norm_ffn_pre, w_up, ffn_conv_w, ffn_conv_b, w_down, norm_ffn_post)
    depth = w_in.shape[0]
    bp, lp, d = x_prompt.shape
    bs, ls, _ = x_sample.shape
    tp = bp * lp
    segs = ((0, bp, lp), (tp, bs, ls))
    xp = x_prompt.reshape(tp, d)
    xs = x_sample.reshape(bs * ls, d)
    dt_in = x_prompt.dtype
    outs_p = ([], [], [], [])
    outs_s = ([], [], [], [])
    for layer in range(depth):
        wl = tuple(w[layer] for w in weights)
        cache_s = (cache_conv_a[layer], cache_conv_b[layer], state_ssd[layer], cache_ffn_conv[layer])
        xp, xs, (new_p, new_s) = _layer(xp, xs, segs, cache_s, wl)
        for lst, v in zip(outs_p, new_p):
            lst.append(v.astype(dt_in))
        for lst, v in zip(outs_s, new_s):
            lst.append(v.astype(dt_in))
    return (xp.reshape(bp, lp, d), xs.reshape(bs, ls, d),
            jnp.stack(outs_p[0]), jnp.stack(outs_p[1]), jnp.stack(outs_p[2]), jnp.stack(outs_p[3]),
            jnp.stack(outs_s[0]), jnp.stack(outs_s[1]), jnp.stack(outs_s[2]), jnp.stack(outs_s[3]))
```

```python
import functools

import jax
import jax.numpy as jnp
from jax import lax
from jax.experimental import pallas as pl
from jax.experimental.pallas import tpu as pltpu

F32 = jnp.float32
BF16 = jnp.bfloat16
EPS = 1e-6
LANES = 128
SUBLANES = 8
VMEM_LIMIT = 56 * 1024 * 1024
SSD_CHUNK = 128
CONV_LANES = 512
NEG_BIG = -1e30
NN_DIMS = (((1,), (0,)), ((), ()))
NT_DIMS = (((1,), (1,)), ((), ()))


def _pick(n, cands):
    for c in cands:
        if n % c == 0:
            return c
    raise ValueError(f"no tile in {cands} divides {n}")


def _cparams(sem):
    return pltpu.CompilerParams(dimension_semantics=sem, vmem_limit_bytes=VMEM_LIMIT)


def _sigmoid(x):
    return 1.0 / (1.0 + jnp.exp(-x))


def _silu(x):
    return x * _sigmoid(x)


def _softplus(x):
    return jnp.maximum(x, 0.0) + jnp.log1p(jnp.exp(-jnp.abs(x)))


def _rms_scale(x, eps):
    return x * lax.rsqrt(jnp.mean(x * x, axis=-1, keepdims=True) + eps)


def _prenorm_kernel(xp_ref, xs_ref, g_ref, o_ref, *, n_p):
    i = pl.program_id(0)

    def run(x_ref):
        o_ref[...] = (_rms_scale(x_ref[...].astype(F32), EPS) * g_ref[...]).astype(o_ref.dtype)

    pl.when(i < n_p)(functools.partial(run, xp_ref))
    pl.when(i >= n_p)(functools.partial(run, xs_ref))


def _prenorm(xp, xs, gain, *, tm):
    tp, d = xp.shape
    ts = xs.shape[0]
    n_p = tp // tm
    return pl.pallas_call(
        functools.partial(_prenorm_kernel, n_p=n_p),
        grid=((tp + ts) // tm,),
        in_specs=[pl.BlockSpec((tm, d), lambda i: (jnp.minimum(i, n_p - 1), 0)),
                  pl.BlockSpec((tm, d), lambda i: (jnp.maximum(i - n_p, 0), 0)),
                  pl.BlockSpec((1, d), lambda i: (0, 0))],
        out_specs=pl.BlockSpec((tm, d), lambda i: (i, 0)),
        out_shape=jax.ShapeDtypeStruct((tp + ts, d), BF16),
        compiler_params=_cparams(("arbitrary",)),
        name="prenorm",
    )(xp, xs, gain)


def _mm_kernel(*refs, order, n_w, cast_w, w_t, lhs_dual, n_rows, blk_dual, out_dual, n_p, epilogue):
    it = iter(refs)
    lhs_refs = tuple(next(it) for _ in range(2 if lhs_dual else 1))
    w_refs = [next(it) for _ in range(n_w)]
    row_refs = [next(it) for _ in range(n_rows)]
    blk_refs = [tuple(next(it) for _ in range(2 if d else 1)) for d in blk_dual]
    out_refs = [tuple(next(it) for _ in range(2 if d else 1)) for d in out_dual]
    i = pl.program_id(1 if order == "col" else 0)
    if cast_w:
        wbuf = next(it)

        @pl.when(i == 0)
        def _():
            c0 = 0
            for wr in w_refs:
                if w_t:
                    wbuf[c0:c0 + wr.shape[0], :] = wr[...].astype(BF16)
                    c0 += wr.shape[0]
                else:
                    wbuf[:, c0:c0 + wr.shape[1]] = wr[...].astype(BF16)
                    c0 += wr.shape[1]

        w_ref = wbuf
    else:
        w_ref = w_refs[0]

    def run(sel):
        pick = lambda pair: pair[sel if len(pair) > 1 else 0]
        r = lax.dot_general(pick(lhs_refs)[...], w_ref[...], NT_DIMS if w_t else NN_DIMS,
                            preferred_element_type=F32)
        epilogue(r, row_refs, [pick(b) for b in blk_refs], [pick(o) for o in out_refs])

    if lhs_dual or any(blk_dual) or any(out_dual):
        pl.when(i < n_p)(functools.partial(run, 0))
        pl.when(i >= n_p)(functools.partial(run, 1))
    else:
        run(0)


def _matmul(lhs, ws, *, nj, order, tm, outs, epilogue, name, rows=(), blks=(), cast_w=False,
            w_t=False, n_p=None, single_buffer_w=False):
    lhs_dual = isinstance(lhs, tuple)
    t = lhs[0].shape[0] + lhs[1].shape[0] if lhs_dual else lhs.shape[0]
    k = lhs[0].shape[1] if lhs_dual else lhs.shape[1]
    ni = t // tm
    if order == "col":
        grid = (nj, ni)
        spec = lambda shape, fn, **kw: pl.BlockSpec(shape, lambda a, b: fn(b, a), **kw)
    else:
        grid = (ni, nj)
        spec = lambda shape, fn, **kw: pl.BlockSpec(shape, lambda a, b: fn(a, b), **kw)
    assert not cast_w or order == "col" or nj == 1
    prompt_row = lambda i: jnp.minimum(i, n_p - 1)
    decode_row = lambda i: jnp.maximum(i - n_p, 0)

    if lhs_dual:
        in_specs = [spec((tm, k), lambda i, j: (prompt_row(i), 0)),
                    spec((tm, k), lambda i, j: (decode_row(i), 0))]
        args = list(lhs)
    else:
        in_specs = [spec((tm, k), lambda i, j: (i, 0))]
        args = [lhs]
    w_kw = {"pipeline_mode": pl.Buffered(1)} if single_buffer_w else {}
    bn = 0
    for arr, bw, off in ws:
        if w_t:
            in_specs.append(spec((bw, k), lambda i, j, off=off: (j + off, 0), **w_kw))
        else:
            in_specs.append(spec((k, bw), lambda i, j, off=off: (0, j + off), **w_kw))
        args.append(arr)
        bn += bw
    for arr, bw, off in rows:
        in_specs.append(spec((1, bw), lambda i, j, off=off: (0, j + off)))
        args.append(arr)
    for arr, bw, off in blks:
        if isinstance(arr, tuple):
            in_specs.append(spec((tm, bw), lambda i, j, off=off: (prompt_row(i), j + off)))
            in_specs.append(spec((tm, bw), lambda i, j, off=off: (decode_row(i), j + off)))
            args.extend(arr)
        else:
            in_specs.append(spec((tm, bw), lambda i, j, off=off: (i, j + off)))
            args.append(arr)
    out_specs, out_shape = [], []
    for cols, bw, dtype, dual in outs:
        if dual:
            out_specs.append(spec((tm, bw), lambda i, j: (prompt_row(i), j)))
            out_specs.append(spec((tm, bw), lambda i, j: (decode_row(i), j)))
            out_shape.append(jax.ShapeDtypeStruct((n_p * tm, cols), dtype))
            out_shape.append(jax.ShapeDtypeStruct(((ni - n_p) * tm, cols), dtype))
        else:
            out_specs.append(spec((tm, bw), lambda i, j: (i, j)))
            out_shape.append(jax.ShapeDtypeStruct((t, cols), dtype))
    scratch = [pltpu.VMEM((bn, k) if w_t else (k, bn), BF16)] if cast_w else []
    res = pl.pallas_call(
        functools.partial(_mm_kernel, order=order, n_w=len(ws), cast_w=cast_w, w_t=w_t,
                          lhs_dual=lhs_dual, n_rows=len(rows),
                          blk_dual=[isinstance(b[0], tuple) for b in blks],
                          out_dual=[o[3] for o in outs], n_p=n_p, epilogue=epilogue),
        grid=grid,
        in_specs=in_specs,
        out_specs=out_specs,
        out_shape=out_shape,
        scratch_shapes=scratch,
        compiler_params=_cparams(("arbitrary", "arbitrary")),
        name=name,
    )(*args)
    return res[0] if len(res) == 1 else res


def _ep_glu(r, rows, blks, outs):
    half = r.shape[1] // 2
    outs[0][...] = (r[:, :half] * _sigmoid(r[:, half:])).astype(outs[0].dtype)


def _ep_silu(r, rows, blks, outs):
    outs[0][...] = _silu(r).astype(outs[0].dtype)


def _ep_store(r, rows, blks, outs):
    outs[0][...] = r.astype(outs[0].dtype)


def _ep_bias_sigmoid(r, rows, blks, outs):
    outs[0][...] = _sigmoid(r + rows[0][...]).astype(outs[0].dtype)


def _ep_dt(r, rows, blks, outs, *, nheads):
    lane = lax.broadcasted_iota(jnp.int32, r.shape, 1)
    outs[0][...] = jnp.where(lane < nheads, _softplus(r + rows[0][...]), 0.0).astype(outs[0].dtype)


def _ep_gate(r, rows, blks, outs):
    outs[0][...] = (r * blks[0][...].astype(F32)).astype(outs[0].dtype)


def _ep_gate_add(r, rows, blks, outs):
    outs[0][...] = (r * blks[0][...].astype(F32) + blks[1][...].astype(F32)).astype(outs[0].dtype)


def _ep_residual_norm(r, rows, blks, outs):
    x1 = blks[0][...] + _rms_scale(r, EPS) * rows[0][...]
    outs[0][...] = x1
    outs[1][...] = (_rms_scale(x1, EPS) * rows[1][...]).astype(outs[1].dtype)


def _ep_residual(r, rows, blks, outs):
    outs[0][...] = blks[0][...] + _rms_scale(r, EPS) * rows[0][...]


def _ffnup_kernel(h_ref, wg_ref, wv_ref, cwg_ref, cwv_ref, cbg_ref, cbv_ref, hg_ref, hv_ref,
                  act_ref, tailp_ref, tails_ref, wbuf, cbuf, sbuf,
                  *, width, tm, bw, sm, n_p, tiles_per_seq, ls):
    hp = SUBLANES
    i = pl.program_id(1)

    @pl.when(i == 0)
    def _():
        wbuf[:, 0:bw] = wg_ref[...].astype(BF16)
        wbuf[:, bw:2 * bw] = wv_ref[...].astype(BF16)

    w = wbuf[...]

    def conv_act(buf, r0, n):
        def conv(c0, cw_ref, cb_ref):
            acc = jnp.zeros((n, bw), F32) + cb_ref[...]
            for k in range(width):
                off = r0 - (width - 1) + k
                acc = acc + buf[off:off + n, c0:c0 + bw] * cw_ref[k:k + 1, :]
            return acc
        return (jax.nn.gelu(conv(0, cwg_ref, cbg_ref)) * conv(bw, cwv_ref, cbv_ref)).astype(act_ref.dtype)

    @pl.when(i < n_p)
    def _():
        @pl.when(i % tiles_per_seq == 0)
        def _():
            cbuf[0:hp, :] = jnp.zeros((hp, 2 * bw), F32)

        for s in range(tm // sm):
            cbuf[hp + s * sm:hp + (s + 1) * sm, :] = jnp.dot(
                h_ref[s * sm:(s + 1) * sm, :], w, preferred_element_type=F32)
            act_ref[s * sm:(s + 1) * sm, :] = conv_act(cbuf, hp + s * sm, sm)

        @pl.when(i % tiles_per_seq == tiles_per_seq - 1)
        def _():
            tailp_ref[0, 0] = cbuf[tm:tm + hp, 0:bw]
            tailp_ref[0, 1] = cbuf[tm:tm + hp, bw:2 * bw]

        cbuf[0:hp, :] = cbuf[tm:tm + hp, :]

    @pl.when(i >= n_p)
    def _():
        for s in range(tm // sm):
            r = jnp.dot(h_ref[s * sm:(s + 1) * sm, :], w, preferred_element_type=F32)
            for q in range(sm // ls):
                seq = s * (sm // ls) + q
                sbuf[0:hp, 0:bw] = hg_ref[seq]
                sbuf[0:hp, bw:2 * bw] = hv_ref[seq]
                rq = r[q * ls:(q + 1) * ls, :]
                sbuf[hp:hp + ls, :] = rq
                act_ref[s * sm + q * ls:s * sm + (q + 1) * ls, :] = conv_act(sbuf, hp, ls)
                tails_ref[seq, 0] = rq[ls - hp:ls, 0:bw]
                tails_ref[seq, 1] = rq[ls - hp:ls, bw:2 * bw]


def _ffn_up_conv(h, w_up, conv_w, conv_b, hist_s, *, tm, n_p, tiles_per_seq, nseq_p, ls, name):
    t, k = h.shape
    f2 = w_up.shape[1]
    dff = f2 // 2
    width = conv_w.shape[0]
    bw = _pick(dff, (512, 256, 128))
    nj = dff // bw
    ni = t // tm
    sm = _pick(tm, (256, 128, 64))
    spt = tm // ls
    nseq_s = hist_s.shape[0]
    assert tm % ls == 0 and sm % ls == 0 and ls >= SUBLANES and width - 1 <= SUBLANES
    decode_tile = lambda i: jnp.maximum(i - n_p, 0)
    prompt_seq = lambda i: jnp.minimum(i, n_p - 1) // tiles_per_seq
    in_specs = [pl.BlockSpec((tm, k), lambda j, i: (i, 0)),
                pl.BlockSpec((k, bw), lambda j, i: (0, j)),
                pl.BlockSpec((k, bw), lambda j, i: (0, nj + j)),
                pl.BlockSpec((width, bw), lambda j, i: (0, j)),
                pl.BlockSpec((width, bw), lambda j, i: (0, nj + j)),
                pl.BlockSpec((1, bw), lambda j, i: (0, j)),
                pl.BlockSpec((1, bw), lambda j, i: (0, nj + j)),
                pl.BlockSpec((spt, SUBLANES, bw), lambda j, i: (decode_tile(i), 0, j)),
                pl.BlockSpec((spt, SUBLANES, bw), lambda j, i: (decode_tile(i), 0, nj + j))]
    out_specs = [pl.BlockSpec((tm, bw), lambda j, i: (i, j)),
                 pl.BlockSpec((1, 2, SUBLANES, bw), lambda j, i: (prompt_seq(i), 0, 0, j)),
                 pl.BlockSpec((spt, 2, SUBLANES, bw), lambda j, i: (decode_tile(i), 0, 0, j))]
    out_shape = [jax.ShapeDtypeStruct((t, dff), BF16),
                 jax.ShapeDtypeStruct((nseq_p, 2, SUBLANES, dff), F32),
                 jax.ShapeDtypeStruct((nseq_s, 2, SUBLANES, dff), F32)]
    return pl.pallas_call(
        functools.partial(_ffnup_kernel, width=width, tm=tm, bw=bw, sm=sm, n_p=n_p,
                          tiles_per_seq=tiles_per_seq, ls=ls),
        grid=(nj, ni),
        in_specs=in_specs,
        out_specs=out_specs,
        out_shape=out_shape,
        scratch_shapes=[pltpu.VMEM((k, 2 * bw), BF16),
                        pltpu.VMEM((SUBLANES + tm, 2 * bw), F32),
                        pltpu.VMEM((SUBLANES + ls, 2 * bw), F32)],
        compiler_params=_cparams(("arbitrary", "arbitrary")),
        name=name,
    )(h, w_up, w_up, conv_w, conv_w, conv_b, conv_b, hist_s, hist_s)


def _conva_kernel(u_ref, hist_ref, w_ref, cb_ref, lg_ref, lb_ref, o_ref, bufp_ref, cv_ref,
                  *, width, hp, ta, rs, rc, lc):
    c = u_ref.shape[1]
    t = pl.program_id(1)
    n_ext = hp + ta

    @pl.when(t == 0)
    def _():
        bufp_ref[0, 0:hp, :] = hist_ref[0]
        bufp_ref[0, n_ext:n_ext + SUBLANES, :] = jnp.zeros((SUBLANES, c), F32)

    bufp_ref[0, hp:hp + ta, :] = u_ref[...].astype(F32)

    def shift_body(r, carry):
        s0 = pl.multiple_of(r * rs, rs)
        win = bufp_ref[0, pl.ds(s0, rs + SUBLANES), :]
        for p in range(1, SUBLANES):
            bufp_ref[p, pl.ds(s0, rs), :] = win[p:p + rs, :]
        return carry

    lax.fori_loop(0, n_ext // rs, shift_body, 0)

    def conv_body(r, carry):
        r0 = pl.multiple_of(r * rc, rc)
        for l0 in range(0, c, lc):
            acc = jnp.zeros((rc, lc), F32)
            for k in range(width):
                off = hp - (width - 1) + k
                q8, p = (off // SUBLANES) * SUBLANES, off % SUBLANES
                xs = bufp_ref[p, pl.ds(r0 + q8, rc), l0:l0 + lc]
                acc = acc + xs * w_ref[k:k + 1, l0:l0 + lc]
            cv_ref[:, l0:l0 + lc] = acc
        v = cv_ref[...] + cb_ref[...]
        mu = jnp.mean(v, axis=-1, keepdims=True)
        vc = v - mu
        var = jnp.mean(vc * vc, axis=-1, keepdims=True)
        y = vc * lax.rsqrt(var + EPS) * lg_ref[...] + lb_ref[...]
        o_ref[pl.ds(r0, rc), :] = _silu(y).astype(o_ref.dtype)
        return carry

    lax.fori_loop(0, ta // rc, conv_body, 0)
    bufp_ref[0, 0:hp, :] = bufp_ref[0, ta:ta + hp, :]


def _conva_call(u, hist, w, cb, lg, lb, *, row0, nseq, seqlen, name):
    c = u.shape[1]
    width = w.shape[0]
    hp = hist.shape[1]
    ta = _pick(seqlen, (256, 128, 64))
    rs = 32
    rc, lc = 32, 128
    assert row0 % ta == 0 and (hp + ta) % rs == 0 and ta % rc == 0 and hp >= width - 1
    nt = seqlen // ta
    blk0 = row0 // ta
    const = lambda b, t: (0, 0)
    return pl.pallas_call(
        functools.partial(_conva_kernel, width=width, hp=hp, ta=ta, rs=rs, rc=rc, lc=lc),
        grid=(nseq, nt),
        in_specs=[pl.BlockSpec((ta, c), lambda b, t: (blk0 + b * nt + t, 0)),
                  pl.BlockSpec((1, hp, c), lambda b, t: (b, 0, 0)),
                  pl.BlockSpec((width, c), const),
                  pl.BlockSpec((1, c), const),
                  pl.BlockSpec((1, c), const),
                  pl.BlockSpec((1, c), const)],
        out_specs=pl.BlockSpec((ta, c), lambda b, t: (b * nt + t, 0)),
        out_shape=jax.ShapeDtypeStruct((nseq * seqlen, c), BF16),
        scratch_shapes=[pltpu.VMEM((SUBLANES, hp + ta + SUBLANES, c), F32), pltpu.VMEM((rc, c), F32)],
        compiler_params=_cparams(("arbitrary", "arbitrary")),
        name=name,
    )(u, hist, w, cb, lg, lb)


def _ssd_kernel(raw_ref, zs_ref, dt_ref, hist_ref, s0_ref, cw_ref, cb_ref, a_ref, dsk_ref, ng_ref, ex_ref,
                y_ref, sout_ref, xbc_ref, hbuf_ref, y3_ref, st_ref, cumt_ref, cml_ref, wgt_ref,
                *, width, q, ngroups, hg, pdim, nstate):
    gw = hg * pdim
    d_inner = ngroups * gw
    c = pl.program_id(1)
    nchunks = pl.num_programs(1)

    @pl.when(c == 0)
    def _():
        st_ref[...] = s0_ref[0]
        hbuf_ref[...] = hist_ref[0]

    for l0 in range(0, raw_ref.shape[1], CONV_LANES):
        cols = slice(l0, l0 + CONV_LANES)
        raw = raw_ref[:, cols].astype(F32)
        e = jnp.concatenate([hbuf_ref[:, cols], raw], axis=0)
        acc = e * cw_ref[0:1, cols]
        for k in range(1, width):
            acc = e * cw_ref[k:k + 1, cols] + pltpu.roll(acc, 1, axis=0)
        v = acc[SUBLANES:, :] + cb_ref[:, cols]
        xbc_ref[:, cols] = (0.5 * v * (1.0 + jnp.tanh(0.5 * v))).astype(xbc_ref.dtype)
        hbuf_ref[:, cols] = raw[q - SUBLANES:q, :]

    dt = dt_ref[...]
    da = dt * a_ref[...]
    ii = lax.broadcasted_iota(jnp.int32, (q, q), 0)
    jj = lax.broadcasted_iota(jnp.int32, (q, q), 1)
    causal = ii >= jj
    tri = jnp.where(causal, 1.0, 0.0).astype(BF16)
    da_hi = da.astype(BF16)
    da_lo = (da - da_hi.astype(F32)).astype(BF16)
    cum = (jnp.dot(tri, da_hi, preferred_element_type=F32)
           + jnp.dot(tri, da_lo, preferred_element_type=F32))
    cum_t = cum.T
    dt_t = dt.T
    last_t = cum_t[:, q - 1:q]
    nh = ngroups * hg
    cumt_ref[...] = cum_t[0:nh, :].reshape(ngroups, hg, q)
    cml_ref[...] = (cum_t - jnp.log(dt_t))[0:nh, :].reshape(ngroups, hg, q)
    wgt_ref[...] = (dt_t * jnp.exp(last_t - cum_t))[0:nh, :].reshape(ngroups, hg, q)

    e_last = jnp.exp(cum[q - SUBLANES:q, :])
    e_hi = e_last.astype(BF16)
    e_lo = (e_last - e_hi.astype(F32)).astype(BF16)
    dec = (jnp.dot(e_hi, ex_ref[...], preferred_element_type=F32)
           + jnp.dot(e_lo, ex_ref[...], preferred_element_type=F32))
    dec_row = dec[SUBLANES - 1:SUBLANES, :]

    lane = lax.broadcasted_iota(jnp.int32, (1, LANES), 1)
    lo_mask = lane < pdim
    b0 = d_inner
    c0 = d_inner + ngroups * nstate

    for g in range(ngroups):
        bg = xbc_ref[:, b0 + g * nstate:b0 + (g + 1) * nstate]
        cg = xbc_ref[:, c0 + g * nstate:c0 + (g + 1) * nstate]
        scores = lax.dot_general(cg, bg, NT_DIMS, preferred_element_type=F32)
        bg_t = bg.astype(F32).T
        cg32 = cg.astype(F32)
        cum_g = cumt_ref[g]
        cml_g = cml_ref[g]
        wg_g = wgt_ref[g]
        for hp2 in range(hg // 2):
            l0 = g * gw + hp2 * LANES
            xpair = xbc_ref[:, l0:l0 + LANES]
            spair = st_ref[g, :, hp2 * LANES:(hp2 + 1) * LANES]
            lhs_parts, rhs_parts, bw_parts, x_parts = [], [], [], []
            for part in range(2):
                hh = 2 * hp2 + part
                mask = lo_mask if part == 0 else jnp.logical_not(lo_mask)
                x_h = jnp.where(mask, xpair, jnp.zeros_like(xpair))
                s_h = jnp.where(mask, spair, 0.0).astype(BF16)
                row = cum_g[hh:hh + 1, :]
                col = jnp.broadcast_to(row, (LANES, q)).T
                seg = col[:, 0:q] - cml_g[hh:hh + 1, :]
                m_h = scores * jnp.exp(jnp.where(causal, seg, NEG_BIG))
                cs_h = cg32 * jnp.exp(col[:, 0:nstate])
                lhs_parts += [m_h.astype(BF16), cs_h.astype(BF16)]
                rhs_parts += [x_h, s_h]
                bw_parts.append((bg_t * wg_g[hh:hh + 1, :]).astype(BF16))
                x_parts.append(x_h)
            ypair = jnp.dot(jnp.concatenate(lhs_parts, axis=1), jnp.concatenate(rhs_parts, axis=0),
                            preferred_element_type=F32)
            upd = jnp.dot(jnp.concatenate(bw_parts, axis=1), jnp.concatenate(x_parts, axis=0),
                          preferred_element_type=F32)
            y3_ref[g, :, hp2 * LANES:(hp2 + 1) * LANES] = ypair
            st_ref[g, :, hp2 * LANES:(hp2 + 1) * LANES] = spair * dec_row[:, l0:l0 + LANES] + upd

    ssq = jnp.zeros((q, 1), F32)
    for g in range(ngroups):
        cols = slice(g * gw, (g + 1) * gw)
        yg = (y3_ref[g] + dsk_ref[:, cols] * xbc_ref[:, cols].astype(F32)) * zs_ref[:, cols].astype(F32)
        y3_ref[g] = yg
        ssq = ssq + jnp.sum(yg * yg, axis=-1, keepdims=True)
    inv = lax.rsqrt(ssq / d_inner + EPS)
    for g in range(ngroups):
        cols = slice(g * gw, (g + 1) * gw)
        y_ref[:, cols] = (y3_ref[g] * inv * ng_ref[:, cols]).astype(y_ref.dtype)

    @pl.when(c == nchunks - 1)
    def _():
        sout_ref[0] = st_ref[...]


def _ssd_call(xbc, zs, dt, hist, s0, conv_w, conv_b, a_row, dsk_row, ng_row, ex, *,
              row0, nseq, seqlen, ngroups, hg, pdim, nstate, name):
    dxbc = xbc.shape[1]
    width = conv_w.shape[0]
    assert dxbc % CONV_LANES == 0 and width - 1 <= SUBLANES
    d_inner = zs.shape[1]
    q = _pick(seqlen, (SSD_CHUNK, 64))
    assert row0 % q == 0 and pdim * 2 == LANES and hg % 2 == 0 and nstate == LANES
    nc = seqlen // q
    blk0 = row0 // q
    gw = hg * pdim
    row_map = lambda b, c: (blk0 + b * nc + c, 0)
    const = lambda b, c: (0, 0)
    seq4 = lambda b, c: (b, 0, 0, 0)
    return pl.pallas_call(
        functools.partial(_ssd_kernel, width=width, q=q, ngroups=ngroups, hg=hg, pdim=pdim, nstate=nstate),
        grid=(nseq, nc),
        in_specs=[pl.BlockSpec((q, dxbc), row_map),
                  pl.BlockSpec((q, d_inner), row_map),
                  pl.BlockSpec((q, LANES), row_map),
                  pl.BlockSpec((1, SUBLANES, dxbc), lambda b, c: (b, 0, 0)),
                  pl.BlockSpec((1, ngroups, nstate, gw), seq4),
                  pl.BlockSpec((width, dxbc), const),
                  pl.BlockSpec((1, dxbc), const),
                  pl.BlockSpec((1, LANES), const),
                  pl.BlockSpec((1, d_inner), const),
                  pl.BlockSpec((1, d_inner), const),
                  pl.BlockSpec((LANES, d_inner), const)],
        out_specs=[pl.BlockSpec((q, d_inner), lambda b, c: (b * nc + c, 0)),
                   pl.BlockSpec((1, ngroups, nstate, gw), seq4)],
        out_shape=[jax.ShapeDtypeStruct((nseq * seqlen, d_inner), BF16),
                   jax.ShapeDtypeStruct((nseq, ngroups, nstate, gw), F32)],
        scratch_shapes=[pltpu.VMEM((q, dxbc), BF16),
                        pltpu.VMEM((SUBLANES, dxbc), F32),
                        pltpu.VMEM((ngroups, q, gw), F32),
                        pltpu.VMEM((ngroups, nstate, gw), F32),
                        pltpu.VMEM((ngroups, hg, q), F32),
                        pltpu.VMEM((ngroups, hg, q), F32),
                        pltpu.VMEM((ngroups, hg, q), F32)],
        compiler_params=_cparams(("arbitrary", "arbitrary")),
        name=name,
    )(xbc, zs, dt, hist, s0, conv_w, conv_b, a_row, dsk_row, ng_row, ex)


def _pad_hist(cache, hp):
    nseq, wm1, c = cache.shape
    return jnp.pad(cache.astype(F32), ((0, 0), (hp - wm1, 0), (0, 0)))


def _tail_rows(rows_all, row0, nseq, seqlen, keep):
    return jnp.stack([lax.slice_in_dim(rows_all, row0 + (b + 1) * seqlen - keep, row0 + (b + 1) * seqlen)
                      for b in range(nseq)]).astype(F32)


def _conv_tails(tail, keep):
    nseq, nw, _, c = tail.shape
    return tail[:, :, SUBLANES - keep:].transpose(0, 2, 1, 3).reshape(nseq, keep, nw * c)


def _layer(xp, xs, segs, cache_s, wts):
    (norm_mix_pre, w_in, b_gate, conv_a_w, conv_a_b, ln_a_g, ln_a_b, w_a_out, conv_b_w, conv_b_b,
     dt_bias, a_log, d_skip, ssd_norm_g, w_b_out, w_o, norm_mix_post, norm_ffn_pre, w_up,
     ffn_conv_w, ffn_conv_b, w_down, norm_ffn_post) = wts
    tp, d = xp.shape
    ts = xs.shape[0]
    (_, nseq_p, len_p), (_, nseq_s, len_s) = segs
    dc = conv_a_w.shape[1]
    wa = conv_a_w.shape[0]
    dxbc = conv_b_w.shape[1]
    wb = conv_b_w.shape[0]
    nheads = a_log.shape[0]
    d_inner = ssd_norm_g.shape[0]
    pdim = d_inner // nheads
    nstate = cache_s[2].shape[-1]
    ngroups = (dxbc - d_inner) // (2 * nstate)
    hg = nheads // ngroups
    gw = hg * pdim
    f2 = ffn_conv_w.shape[1]
    dff = f2 // 2
    wf = ffn_conv_w.shape[0]
    assert nheads <= LANES and min(len_p, len_s) >= max(wa, wb, wf) - 1

    row = lambda v: v.reshape(1, -1).astype(F32)
    tm = _pick(ts, (1024, 512, 256, 128, 64))
    assert tp % tm == 0 and len_p % tm == 0
    n_p = tp // tm
    seq_kw = dict(tm=tm, n_p=n_p, tiles_per_seq=len_p // tm, nseq_p=nseq_p, ls=len_s)

    h = _prenorm(xp, xs, row(norm_mix_pre), tm=tm)

    w_in_t = jnp.swapaxes(w_in, 0, 1)
    o_val, o_gate, o_z, o_xbc = 0, dc, 2 * dc, 2 * dc + d_inner
    o_dt = o_xbc + dxbc
    o_g = o_dt + nheads
    bh = _pick(dc, (512, 256, 128))
    u = _matmul(h, [(w_in_t, bh, o_val // bh), (w_in_t, bh, o_gate // bh)], nj=dc // bh, order="col", tm=tm,
                outs=[(dc, bh, BF16, False)], epilogue=_ep_glu, cast_w=True, w_t=True, name="inproj_glu")
    bz = _pick(d_inner, (1024, 512, 256, 128))
    assert o_z % bz == 0
    zs = _matmul(h, [(w_in_t, bz, o_z // bz)], nj=d_inner // bz, order="col", tm=tm,
                 outs=[(d_inner, bz, BF16, False)], epilogue=_ep_silu, cast_w=True, w_t=True, name="inproj_z")
    bx = _pick(dxbc, (1024, 512, 256, 128))
    assert o_xbc % bx == 0
    xbc = _matmul(h, [(w_in_t, bx, o_xbc // bx)], nj=dxbc // bx, order="col", tm=tm,
                  outs=[(dxbc, bx, BF16, False)], epilogue=_ep_store, cast_w=True, w_t=True, name="inproj_xbc")
    assert o_dt % LANES == 0 and o_dt + LANES <= w_in_t.shape[0]
    dt_b = jnp.pad(dt_bias.astype(F32), (0, LANES - nheads)).reshape(1, LANES)
    dt = _matmul(h, [(w_in_t, LANES, o_dt // LANES)], nj=1, order="col", tm=tm,
                 outs=[(LANES, LANES, F32, False)], epilogue=functools.partial(_ep_dt, nheads=nheads),
                 rows=[(dt_b, LANES, 0)], cast_w=True, w_t=True, name="inproj_dt")
    bg = _pick(2 * d, (1024, 512, 256, 128))
    gates = _matmul(h, [(w_in_t[o_g:].astype(BF16), bg, 0)], nj=2 * d // bg, order="col", tm=tm,
                    outs=[(2 * d, bg, BF16, False)], epilogue=_ep_bias_sigmoid, w_t=True,
                    rows=[(row(b_gate), bg, 0)], name="inproj_gates")

    hp_a = -(-(wa - 1) // SUBLANES) * SUBLANES
    hist_a = [jnp.zeros((nseq_p, hp_a, dc), F32), _pad_hist(cache_s[0], hp_a)]
    a_act = tuple(_conva_call(u, hist_a[si], conv_a_w.astype(F32), row(conv_a_b), row(ln_a_g), row(ln_a_b),
                              row0=row0, nseq=nseq, seqlen=seqlen, name=f"conva_{si}")
                  for si, (row0, nseq, seqlen) in enumerate(segs))
    bd = _pick(d, (1024, 512, 256, 128))
    ya = _matmul(a_act, [(w_a_out.astype(BF16), bd, 0)], nj=d // bd, order="row", tm=tm, n_p=n_p,
                 outs=[(d, bd, BF16, False)], epilogue=_ep_gate, blks=[(gates, bd, 0)], name="proj_a_out")

    a_row = jnp.pad(-jnp.exp(a_log.astype(F32)), (0, LANES - nheads)).reshape(1, LANES)
    dsk_row = jnp.repeat(d_skip.astype(F32), pdim).reshape(1, d_inner)
    ex = (jnp.arange(LANES)[:, None] == (jnp.arange(d_inner)[None, :] // pdim)).astype(BF16)
    s0 = [jnp.zeros((nseq_p, ngroups, nstate, gw), F32),
          cache_s[2].astype(F32).reshape(nseq_s, ngroups, gw, nstate).transpose(0, 1, 3, 2)]
    hist_b = [jnp.zeros((nseq_p, SUBLANES, dxbc), F32), _pad_hist(cache_s[1], SUBLANES)]
    yn, new_states = [], []
    for si, (row0, nseq, seqlen) in enumerate(segs):
        y_si, s_new = _ssd_call(xbc, zs, dt, hist_b[si], s0[si], conv_b_w.astype(F32), row(conv_b_b),
                                a_row, dsk_row, row(ssd_norm_g), ex,
                                row0=row0, nseq=nseq, seqlen=seqlen, ngroups=ngroups, hg=hg,
                                pdim=pdim, nstate=nstate, name=f"ssd_{si}")
        yn.append(y_si)
        new_states.append(s_new.transpose(0, 1, 3, 2).reshape(nseq, nheads, pdim, nstate))
    tm_b = _pick(tm, (512, 256, 128, 64))
    merged = _matmul(tuple(yn), [(w_b_out.astype(BF16), bd, 0)], nj=d // bd, order="row", tm=tm_b,
                     n_p=tp // tm_b, outs=[(d, bd, BF16, False)], epilogue=_ep_gate_add,
                     blks=[(gates, bd, d // bd), (ya, bd, 0)], name="proj_b_out")
    tm_o = _pick(tm, (256, 128, 64))
    x1, h2 = _matmul(merged, [(w_o.astype(BF16), d, 0)], nj=1, order="row", tm=tm_o,
                     outs=[(d, d, F32, False), (d, d, BF16, False)], epilogue=_ep_residual_norm,
                     rows=[(row(norm_mix_post), d, 0), (row(norm_ffn_pre), d, 0)],
                     blks=[((xp, xs), d, 0)], n_p=tp // tm_o, single_buffer_w=True, name="proj_o")

    act, ffn_tail_p, ffn_tail_s = _ffn_up_conv(h2, w_up, ffn_conv_w.astype(F32), row(ffn_conv_b),
                                               _pad_hist(cache_s[3], SUBLANES), name="ffn_up", **seq_kw)
    yp, ys = _matmul(act, [(w_down.astype(BF16), d, 0)], nj=1, order="row", tm=tm_o,
                     outs=[(d, d, F32, True)], epilogue=_ep_residual,
                     rows=[(row(norm_ffn_post), d, 0)], blks=[(x1, d, 0)], n_p=tp // tm_o,
                     single_buffer_w=True, name="ffn_down")

    new_caches = []
    for si, (row0, nseq, seqlen) in enumerate(segs):
        new_caches.append((_tail_rows(u, row0, nseq, seqlen, wa - 1),
                           _tail_rows(xbc, row0, nseq, seqlen, wb - 1),
                           new_states[si],
                           _conv_tails((ffn_tail_p, ffn_tail_s)[si], wf - 1)))
    return yp, ys, new_caches


def kernel(x_prompt, x_sample, cache_conv_a, cache_conv_b, state_ssd, cache_ffn_conv, norm_mix_pre, w_in, b_gate, conv_a_w, conv_a_b, ln_a_g, ln_a_b, w_a_out, conv_b_w, conv_b_b, dt_bias, a_log, d_skip, ssd_norm_g, w_b_out, w_o, norm_mix_post, norm_ffn_pre, w_up, ffn_conv_w, ffn_conv_b, w_down, norm_ffn_post):
    weights = (norm_mix_pre, w_in, b_gate, conv_a_w, conv_a_b, ln_a_g, ln_a_b, w_a_out, conv_b_w,
               conv_b_b, dt_bias, a_log, d_skip, ssd_norm_g, w_b_out, w_o, norm_mix_post,
               norm_ffn_pre, w_up, ffn_conv_w, ffn_conv_b, w_down, norm_ffn_post)
    depth = w_in.shape[0]
    bp, lp, d = x_prompt.shape
    bs, ls, _ = x_sample.shape
    tp = bp * lp
    segs = ((0, bp, lp), (tp, bs, ls))
    xp = x_prompt.reshape(tp, d)
    xs = x_sample.reshape(bs * ls, d)
    dt_in = x_prompt.dtype
    outs_p = ([], [], [], [])
    outs_s = ([], [], [], [])
    for layer in range(depth):
        wl = tuple(w[layer] for w in weights)
        cache_s = (cache_conv_a[layer], cache_conv_b[layer], state_ssd[layer], cache_ffn_conv[layer])
        xp, xs, (new_p, new_s) = _layer(xp, xs, segs, cache_s, wl)
        for lst, v in zip(outs_p, new_p):
            lst.append(v.astype(dt_in))
        for lst, v in zip(outs_s, new_s):
            lst.append(v.astype(dt_in))
    return (xp.reshape(bp, lp, d), xs.reshape(bs, ls, d),
            jnp.stack(outs_p[0]), jnp.stack(outs_p[1]), jnp.stack(outs_p[2]), jnp.stack(outs_p[3]),
            jnp.stack(outs_s[0]), jnp.stack(outs_s[1]), jnp.stack(outs_s[2]), jnp.stack(outs_s[3]))
```

```python
import functools

import jax
import jax.numpy as jnp
from jax import lax
from jax.experimental import pallas as pl
from jax.experimental.pallas import tpu as pltpu

F32 = jnp.float32
BF16 = jnp.bfloat16
EPS = 1e-6
LANES = 128
SUBLANES = 8
VMEM_LIMIT = 56 * 1024 * 1024
SSD_CHUNK = 128
CONV_LANES = 512
NEG_BIG = -1e30
NN_DIMS = (((1,), (0,)), ((), ()))
NT_DIMS = (((1,), (1,)), ((), ()))


def _pick(n, cands):
    for c in cands:
        if n % c == 0:
            return c
    raise ValueError(f"no tile in {cands} divides {n}")


def _cparams(sem):
    return pltpu.CompilerParams(dimension_semantics=sem, vmem_limit_bytes=VMEM_LIMIT)


def _sigmoid(x):
    return 1.0 / (1.0 + jnp.exp(-x))


def _silu(x):
    return x * _sigmoid(x)


def _softplus(x):
    return jnp.maximum(x, 0.0) + jnp.log1p(jnp.exp(-jnp.abs(x)))


def _rms_scale(x, eps):
    return x * lax.rsqrt(jnp.mean(x * x, axis=-1, keepdims=True) + eps)


def _prenorm_kernel(xp_ref, xs_ref, g_ref, o_ref, *, n_p):
    i = pl.program_id(0)

    def run(x_ref):
        o_ref[...] = (_rms_scale(x_ref[...].astype(F32), EPS) * g_ref[...]).astype(o_ref.dtype)

    pl.when(i < n_p)(functools.partial(run, xp_ref))
    pl.when(i >= n_p)(functools.partial(run, xs_ref))


def _prenorm(xp, xs, gain, *, tm):
    tp, d = xp.shape
    ts = xs.shape[0]
    n_p = tp // tm
    return pl.pallas_call(
        functools.partial(_prenorm_kernel, n_p=n_p),
        grid=((tp + ts) // tm,),
        in_specs=[pl.BlockSpec((tm, d), lambda i: (jnp.minimum(i, n_p - 1), 0)),
                  pl.BlockSpec((tm, d), lambda i: (jnp.maximum(i - n_p, 0), 0)),
                  pl.BlockSpec((1, d), lambda i: (0, 0))],
        out_specs=pl.BlockSpec((tm, d), lambda i: (i, 0)),
        out_shape=jax.ShapeDtypeStruct((tp + ts, d), BF16),
        compiler_params=_cparams(("arbitrary",)),
        name="prenorm",
    )(xp, xs, gain)


def _mm_kernel(*refs, order, n_w, cast_w, w_t, lhs_dual, n_rows, blk_dual, out_dual, n_p, epilogue):
    it = iter(refs)
    lhs_refs = tuple(next(it) for _ in range(2 if lhs_dual else 1))
    w_refs = [next(it) for _ in range(n_w)]
    row_refs = [next(it) for _ in range(n_rows)]
    blk_refs = [tuple(next(it) for _ in range(2 if d else 1)) for d in blk_dual]
    out_refs = [tuple(next(it) for _ in range(2 if d else 1)) for d in out_dual]
    i = pl.program_id(1 if order == "col" else 0)
    if cast_w:
        wbuf = next(it)

        @pl.when(i == 0)
        def _():
            c0 = 0
            for wr in w_refs:
                if w_t:
                    wbuf[c0:c0 + wr.shape[0], :] = wr[...].astype(BF16)
                    c0 += wr.shape[0]
                else:
                    wbuf[:, c0:c0 + wr.shape[1]] = wr[...].astype(BF16)
                    c0 += wr.shape[1]

        w_ref = wbuf
    else:
        w_ref = w_refs[0]

    def run(sel):
        pick = lambda pair: pair[sel if len(pair) > 1 else 0]
        r = lax.dot_general(pick(lhs_refs)[...], w_ref[...], NT_DIMS if w_t else NN_DIMS,
                            preferred_element_type=F32)
        epilogue(r, row_refs, [pick(b) for b in blk_refs], [pick(o) for o in out_refs])

    if lhs_dual or any(blk_dual) or any(out_dual):
        pl.when(i < n_p)(functools.partial(run, 0))
        pl.when(i >= n_p)(functools.partial(run, 1))
    else:
        run(0)


def _matmul(lhs, ws, *, nj, order, tm, outs, epilogue, name, rows=(), blks=(), cast_w=False,
            w_t=False, n_p=None, single_buffer_w=False):
    lhs_dual = isinstance(lhs, tuple)
    t = lhs[0].shape[0] + lhs[1].shape[0] if lhs_dual else lhs.shape[0]
    k = lhs[0].shape[1] if lhs_dual else lhs.shape[1]
    ni = t // tm
    if order == "col":
        grid = (nj, ni)
        spec = lambda shape, fn, **kw: pl.BlockSpec(shape, lambda a, b: fn(b, a), **kw)
    else:
        grid = (ni, nj)
        spec = lambda shape, fn, **kw: pl.BlockSpec(shape, lambda a, b: fn(a, b), **kw)
    assert not cast_w or order == "col" or nj == 1
    prompt_row = lambda i: jnp.minimum(i, n_p - 1)
    decode_row = lambda i: jnp.maximum(i - n_p, 0)

    if lhs_dual:
        in_specs = [spec((tm, k), lambda i, j: (prompt_row(i), 0)),
                    spec((tm, k), lambda i, j: (decode_row(i), 0))]
        args = list(lhs)
    else:
        in_specs = [spec((tm, k), lambda i, j: (i, 0))]
        args = [lhs]
    w_kw = {"pipeline_mode": pl.Buffered(1)} if single_buffer_w else {}
    bn = 0
    for arr, bw, off in ws:
        if w_t:
            in_specs.append(spec((bw, k), lambda i, j, off=off: (j + off, 0), **w_kw))
        else:
            in_specs.append(spec((k, bw), lambda i, j, off=off: (0, j + off), **w_kw))
        args.append(arr)
        bn += bw
    for arr, bw, off in rows:
        in_specs.append(spec((1, bw), lambda i, j, off=off: (0, j + off)))
        args.append(arr)
    for arr, bw, off in blks:
        if isinstance(arr, tuple):
            in_specs.append(spec((tm, bw), lambda i, j, off=off: (prompt_row(i), j + off)))
            in_specs.append(spec((tm, bw), lambda i, j, off=off: (decode_row(i), j + off)))
            args.extend(arr)
        else:
            in_specs.append(spec((tm, bw), lambda i, j, off=off: (i, j + off)))
            args.append(arr)
    out_specs, out_shape = [], []
    for cols, bw, dtype, dual in outs:
        if dual:
            out_specs.append(spec((tm, bw), lambda i, j: (prompt_row(i), j)))
            out_specs.append(spec((tm, bw), lambda i, j: (decode_row(i), j)))
            out_shape.append(jax.ShapeDtypeStruct((n_p * tm, cols), dtype))
            out_shape.append(jax.ShapeDtypeStruct(((ni - n_p) * tm, cols), dtype))
        else:
            out_specs.append(spec((tm, bw), lambda i, j: (i, j)))
            out_shape.append(jax.ShapeDtypeStruct((t, cols), dtype))
    scratch = [pltpu.VMEM((bn, k) if w_t else (k, bn), BF16)] if cast_w else []
    res = pl.pallas_call(
        functools.partial(_mm_kernel, order=order, n_w=len(ws), cast_w=cast_w, w_t=w_t,
                          lhs_dual=lhs_dual, n_rows=len(rows),
                          blk_dual=[isinstance(b[0], tuple) for b in blks],
                          out_dual=[o[3] for o in outs], n_p=n_p, epilogue=epilogue),
        grid=grid,
        in_specs=in_specs,
        out_specs=out_specs,
        out_shape=out_shape,
        scratch_shapes=scratch,
        compiler_params=_cparams(("arbitrary", "arbitrary")),
        name=name,
    )(*args)
    return res[0] if len(res) == 1 else res


def _ep_glu(r, rows, blks, outs):
    half = r.shape[1] // 2
    outs[0][...] = (r[:, :half] * _sigmoid(r[:, half:])).astype(outs[0].dtype)


def _ep_silu(r, rows, blks, outs):
    outs[0][...] = _silu(r).astype(outs[0].dtype)


def _ep_store(r, rows, blks, outs):
    outs[0][...] = r.astype(outs[0].dtype)


def _ep_bias_sigmoid(r, rows, blks, outs):
    outs[0][...] = _sigmoid(r + rows[0][...]).astype(outs[0].dtype)


def _ep_dt(r, rows, blks, outs, *, nheads):
    lane = lax.broadcasted_iota(jnp.int32, r.shape, 1)
    outs[0][...] = jnp.where(lane < nheads, _softplus(r + rows[0][...]), 0.0).astype(outs[0].dtype)


def _ep_gate(r, rows, blks, outs):
    outs[0][...] = (r * blks[0][...].astype(F32)).astype(outs[0].dtype)


def _ep_gate_add(r, rows, blks, outs):
    outs[0][...] = (r * blks[0][...].astype(F32) + blks[1][...].astype(F32)).astype(outs[0].dtype)


def _ep_residual_norm(r, rows, blks, outs):
    x1 = blks[0][...] + _rms_scale(r, EPS) * rows[0][...]
    outs[0][...] = x1
    outs[1][...] = (_rms_scale(x1, EPS) * rows[1][...]).astype(outs[1].dtype)


def _ep_residual(r, rows, blks, outs):
    outs[0][...] = blks[0][...] + _rms_scale(r, EPS) * rows[0][...]


def _ffnup_kernel(h_ref, wg_ref, wv_ref, cwg_ref, cwv_ref, cbg_ref, cbv_ref, hg_ref, hv_ref,
                  act_ref, tailp_ref, tails_ref, wbuf, cbuf, sbuf,
                  *, width, tm, bw, sm, n_p, tiles_per_seq, ls):
    hp = SUBLANES
    i = pl.program_id(1)

    @pl.when(i == 0)
    def _():
        wbuf[:, 0:bw] = wg_ref[...].astype(BF16)
        wbuf[:, bw:2 * bw] = wv_ref[...].astype(BF16)

    w = wbuf[...]

    def conv_act(buf, r0, n):
        def conv(c0, cw_ref, cb_ref):
            acc = jnp.zeros((n, bw), F32) + cb_ref[...]
            for k in range(width):
                off = r0 - (width - 1) + k
                acc = acc + buf[off:off + n, c0:c0 + bw] * cw_ref[k:k + 1, :]
            return acc
        return (jax.nn.gelu(conv(0, cwg_ref, cbg_ref)) * conv(bw, cwv_ref, cbv_ref)).astype(act_ref.dtype)

    @pl.when(i < n_p)
    def _():
        @pl.when(i % tiles_per_seq == 0)
        def _():
            cbuf[0:hp, :] = jnp.zeros((hp, 2 * bw), F32)

        for s in range(tm // sm):
            cbuf[hp + s * sm:hp + (s + 1) * sm, :] = jnp.dot(
                h_ref[s * sm:(s + 1) * sm, :], w, preferred_element_type=F32)
            act_ref[s * sm:(s + 1) * sm, :] = conv_act(cbuf, hp + s * sm, sm)

        @pl.when(i % tiles_per_seq == tiles_per_seq - 1)
        def _():
            tailp_ref[0, 0] = cbuf[tm:tm + hp, 0:bw]
            tailp_ref[0, 1] = cbuf[tm:tm + hp, bw:2 * bw]

        cbuf[0:hp, :] = cbuf[tm:tm + hp, :]

    @pl.when(i >= n_p)
    def _():
        for s in range(tm // sm):
            r = jnp.dot(h_ref[s * sm:(s + 1) * sm, :], w, preferred_element_type=F32)
            for q in range(sm // ls):
                seq = s * (sm // ls) + q
                sbuf[0:hp, 0:bw] = hg_ref[seq]
                sbuf[0:hp, bw:2 * bw] = hv_ref[seq]
                rq = r[q * ls:(q + 1) * ls, :]
                sbuf[hp:hp + ls, :] = rq
                act_ref[s * sm + q * ls:s * sm + (q + 1) * ls, :] = conv_act(sbuf, hp, ls)
                tails_ref[seq, 0] = rq[ls - hp:ls, 0:bw]
                tails_ref[seq, 1] = rq[ls - hp:ls, bw:2 * bw]


def _ffn_up_conv(h, w_up, conv_w, conv_b, hist_s, *, tm, n_p, tiles_per_seq, nseq_p, ls, name):
    t, k = h.shape
    f2 = w_up.shape[1]
    dff = f2 // 2
    width = conv_w.shape[0]
    bw = _pick(dff, (512, 256, 128))
    nj = dff // bw
    ni = t // tm
    sm = _pick(tm, (256, 128, 64))
    spt = tm // ls
    nseq_s = hist_s.shape[0]
    assert tm % ls == 0 and sm % ls == 0 and ls >= SUBLANES and width - 1 <= SUBLANES
    decode_tile = lambda i: jnp.maximum(i - n_p, 0)
    prompt_seq = lambda i: jnp.minimum(i, n_p - 1) // tiles_per_seq
    in_specs = [pl.BlockSpec((tm, k), lambda j, i: (i, 0)),
                pl.BlockSpec((k, bw), lambda j, i: (0, j)),
                pl.BlockSpec((k, bw), lambda j, i: (0, nj + j)),
                pl.BlockSpec((width, bw), lambda j, i: (0, j)),
                pl.BlockSpec((width, bw), lambda j, i: (0, nj + j)),
                pl.BlockSpec((1, bw), lambda j, i: (0, j)),
                pl.BlockSpec((1, bw), lambda j, i: (0, nj + j)),
                pl.BlockSpec((spt, SUBLANES, bw), lambda j, i: (decode_tile(i), 0, j)),
                pl.BlockSpec((spt, SUBLANES, bw), lambda j, i: (decode_tile(i), 0, nj + j))]
    out_specs = [pl.BlockSpec((tm, bw), lambda j, i: (i, j)),
                 pl.BlockSpec((1, 2, SUBLANES, bw), lambda j, i: (prompt_seq(i), 0, 0, j)),
                 pl.BlockSpec((spt, 2, SUBLANES, bw), lambda j, i: (decode_tile(i), 0, 0, j))]
    out_shape = [jax.ShapeDtypeStruct((t, dff), BF16),
                 jax.ShapeDtypeStruct((nseq_p, 2, SUBLANES, dff), F32),
                 jax.ShapeDtypeStruct((nseq_s, 2, SUBLANES, dff), F32)]
    return pl.pallas_call(
        functools.partial(_ffnup_kernel, width=width, tm=tm, bw=bw, sm=sm, n_p=n_p,
                          tiles_per_seq=tiles_per_seq, ls=ls),
        grid=(nj, ni),
        in_specs=in_specs,
        out_specs=out_specs,
        out_shape=out_shape,
        scratch_shapes=[pltpu.VMEM((k, 2 * bw), BF16),
                        pltpu.VMEM((SUBLANES + tm, 2 * bw), F32),
                        pltpu.VMEM((SUBLANES + ls, 2 * bw), F32)],
        compiler_params=_cparams(("arbitrary", "arbitrary")),
        name=name,
    )(h, w_up, w_up, conv_w, conv_w, conv_b, conv_b, hist_s, hist_s)


def _conva_kernel(u_ref, hist_ref, w_ref, cb_ref, lg_ref, lb_ref, o_ref, bufp_ref, cv_ref,
                  *, width, hp, ta, rs, rc, lc):
    c = u_ref.shape[1]
    t = pl.program_id(1)
    n_ext = hp + ta

    @pl.when(t == 0)
    def _():
        bufp_ref[0, 0:hp, :] = hist_ref[0]
        bufp_ref[0, n_ext:n_ext + SUBLANES, :] = jnp.zeros((SUBLANES, c), F32)

    bufp_ref[0, hp:hp + ta, :] = u_ref[...].astype(F32)

    def shift_body(r, carry):
        s0 = pl.multiple_of(r * rs, rs)
        win = bufp_ref[0, pl.ds(s0, rs + SUBLANES), :]
        for p in range(1, SUBLANES):
            bufp_ref[p, pl.ds(s0, rs), :] = win[p:p + rs, :]
        return carry

    lax.fori_loop(0, n_ext // rs, shift_body, 0)

    def conv_body(r, carry):
        r0 = pl.multiple_of(r * rc, rc)
        for l0 in range(0, c, lc):
            acc = jnp.zeros((rc, lc), F32)
            for k in range(width):
                off = hp - (width - 1) + k
                q8, p = (off // SUBLANES) * SUBLANES, off % SUBLANES
                xs = bufp_ref[p, pl.ds(r0 + q8, rc), l0:l0 + lc]
                acc = acc + xs * w_ref[k:k + 1, l0:l0 + lc]
            cv_ref[:, l0:l0 + lc] = acc
        v = cv_ref[...] + cb_ref[...]
        mu = jnp.mean(v, axis=-1, keepdims=True)
        vc = v - mu
        var = jnp.mean(vc * vc, axis=-1, keepdims=True)
        y = vc * lax.rsqrt(var + EPS) * lg_ref[...] + lb_ref[...]
        o_ref[pl.ds(r0, rc), :] = _silu(y).astype(o_ref.dtype)
        return carry

    lax.fori_loop(0, ta // rc, conv_body, 0)
    bufp_ref[0, 0:hp, :] = bufp_ref[0, ta:ta + hp, :]


def _conva_call(u, hist, w, cb, lg, lb, *, row0, nseq, seqlen, name):
    c = u.shape[1]
    width = w.shape[0]
    hp = hist.shape[1]
    ta = _pick(seqlen, (256, 128, 64))
    rs = 32
    rc, lc = 32, 128
    assert row0 % ta == 0 and (hp + ta) % rs == 0 and ta % rc == 0 and hp >= width - 1
    nt = seqlen // ta
    blk0 = row0 // ta
    const = lambda b, t: (0, 0)
    return pl.pallas_call(
        functools.partial(_conva_kernel, width=width, hp=hp, ta=ta, rs=rs, rc=rc, lc=lc),
        grid=(nseq, nt),
        in_specs=[pl.BlockSpec((ta, c), lambda b, t: (blk0 + b * nt + t, 0)),
                  pl.BlockSpec((1, hp, c), lambda b, t: (b, 0, 0)),
                  pl.BlockSpec((width, c), const),
                  pl.BlockSpec((1, c), const),
                  pl.BlockSpec((1, c), const),
                  pl.BlockSpec((1, c), const)],
        out_specs=pl.BlockSpec((ta, c), lambda b, t: (b * nt + t, 0)),
        out_shape=jax.ShapeDtypeStruct((nseq * seqlen, c), BF16),
        scratch_shapes=[pltpu.VMEM((SUBLANES, hp + ta + SUBLANES, c), F32), pltpu.VMEM((rc, c), F32)],
        compiler_params=_cparams(("arbitrary", "arbitrary")),
        name=name,
    )(u, hist, w, cb, lg, lb)


def _ssd_kernel(raw_ref, zs_ref, dt_ref, hist_ref, s0_ref, cw_ref, cb_ref, a_ref, dsk_ref, ng_ref, ex_ref,
                y_ref, sout_ref, xbc_ref, hbuf_ref, y3_ref, st_ref, cumt_ref, cml_ref, wgt_ref,
                *, width, q, ngroups, hg, pdim, nstate):
    gw = hg * pdim
    d_inner = ngroups * gw
    c = pl.program_id(1)
    nchunks = pl.num_programs(1)

    @pl.when(c == 0)
    def _():
        st_ref[...] = s0_ref[0]
        hbuf_ref[...] = hist_ref[0]

    for l0 in range(0, raw_ref.shape[1], CONV_LANES):
        cols = slice(l0, l0 + CONV_LANES)
        raw = raw_ref[:, cols].astype(F32)
        e = jnp.concatenate([hbuf_ref[:, cols], raw], axis=0)
        acc = e * cw_ref[0:1, cols]
        for k in range(1, width):
            acc = e * cw_ref[k:k + 1, cols] + pltpu.roll(acc, 1, axis=0)
        v = acc[SUBLANES:, :] + cb_ref[:, cols]
        xbc_ref[:, cols] = (0.5 * v * (1.0 + jnp.tanh(0.5 * v))).astype(xbc_ref.dtype)
        hbuf_ref[:, cols] = raw[q - SUBLANES:q, :]

    dt = dt_ref[...]
    da = dt * a_ref[...]
    ii = lax.broadcasted_iota(jnp.int32, (q, q), 0)
    jj = lax.broadcasted_iota(jnp.int32, (q, q), 1)
    causal = ii >= jj
    tri = jnp.where(causal, 1.0, 0.0).astype(BF16)
    da_hi = da.astype(BF16)
    da_lo = (da - da_hi.astype(F32)).astype(BF16)
    cum = (jnp.dot(tri, da_hi, preferred_element_type=F32)
           + jnp.dot(tri, da_lo, preferred_element_type=F32))
    cum_t = cum.T
    dt_t = dt.T
    last_t = cum_t[:, q - 1:q]
    nh = ngroups * hg
    cumt_ref[...] = cum_t[0:nh, :].reshape(ngroups, hg, q)
    cml_ref[...] = (cum_t - jnp.log(dt_t))[0:nh, :].reshape(ngroups, hg, q)
    wgt_ref[...] = (dt_t * jnp.exp(last_t - cum_t))[0:nh, :].reshape(ngroups, hg, q)

    e_last = jnp.exp(cum[q - SUBLANES:q, :])
    e_hi = e_last.astype(BF16)
    e_lo = (e_last - e_hi.astype(F32)).astype(BF16)
    dec = (jnp.dot(e_hi, ex_ref[...], preferred_element_type=F32)
           + jnp.dot(e_lo, ex_ref[...], preferred_element_type=F32))
    dec_row = dec[SUBLANES - 1:SUBLANES, :]

    lane = lax.broadcasted_iota(jnp.int32, (1, LANES), 1)
    lo_mask = lane < pdim
    b0 = d_inner
    c0 = d_inner + ngroups * nstate

    for g in range(ngroups):
        bg = xbc_ref[:, b0 + g * nstate:b0 + (g + 1) * nstate]
        cg = xbc_ref[:, c0 + g * nstate:c0 + (g + 1) * nstate]
        scores = lax.dot_general(cg, bg, NT_DIMS, preferred_element_type=F32)
        bg_t = bg.astype(F32).T
        cg32 = cg.astype(F32)
        cum_g = cumt_ref[g]
        cml_g = cml_ref[g]
        wg_g = wgt_ref[g]
        for hp2 in range(hg // 2):
            l0 = g * gw + hp2 * LANES
            xpair = xbc_ref[:, l0:l0 + LANES]
            spair = st_ref[g, :, hp2 * LANES:(hp2 + 1) * LANES]
            lhs_parts, rhs_parts, bw_parts, x_parts = [], [], [], []
            for part in range(2):
                hh = 2 * hp2 + part
                mask = lo_mask if part == 0 else jnp.logical_not(lo_mask)
                x_h = jnp.where(mask, xpair, jnp.zeros_like(xpair))
                s_h = jnp.where(mask, spair, 0.0).astype(BF16)
                row = cum_g[hh:hh + 1, :]
                col = jnp.broadcast_to(row, (LANES, q)).T
                seg = col[:, 0:q] - cml_g[hh:hh + 1, :]
                m_h = scores * jnp.exp(jnp.where(causal, seg, NEG_BIG))
                cs_h = cg32 * jnp.exp(col[:, 0:nstate])
                lhs_parts += [m_h.astype(BF16), cs_h.astype(BF16)]
                rhs_parts += [x_h, s_h]
                bw_parts.append((bg_t * wg_g[hh:hh + 1, :]).astype(BF16))
                x_parts.append(x_h)
            ypair = jnp.dot(jnp.concatenate(lhs_parts, axis=1), jnp.concatenate(rhs_parts, axis=0),
                            preferred_element_type=F32)
            upd = jnp.dot(jnp.concatenate(bw_parts, axis=1), jnp.concatenate(x_parts, axis=0),
                          preferred_element_type=F32)
            y3_ref[g, :, hp2 * LANES:(hp2 + 1) * LANES] = ypair
            st_ref[g, :, hp2 * LANES:(hp2 + 1) * LANES] = spair * dec_row[:, l0:l0 + LANES] + upd

    ssq = jnp.zeros((q, 1), F32)
    for g in range(ngroups):
        cols = slice(g * gw, (g + 1) * gw)
        yg = (y3_ref[g] + dsk_ref[:, cols] * xbc_ref[:, cols].astype(F32)) * zs_ref[:, cols].astype(F32)
        y3_ref[g] = yg
        ssq = ssq + jnp.sum(yg * yg, axis=-1, keepdims=True)
    inv = lax.rsqrt(ssq / d_inner + EPS)
    for g in range(ngroups):
        cols = slice(g * gw, (g + 1) * gw)
        y_ref[:, cols] = (y3_ref[g] * inv * ng_ref[:, cols]).astype(y_ref.dtype)

    @pl.when(c == nchunks - 1)
    def _():
        sout_ref[0] = st_ref[...]


def _ssd_call(xbc, zs, dt, hist, s0, conv_w, conv_b, a_row, dsk_row, ng_row, ex, *,
              row0, nseq, seqlen, ngroups, hg, pdim, nstate, name):
    dxbc = xbc.shape[1]
    width = conv_w.shape[0]
    assert dxbc % CONV_LANES == 0 and width - 1 <= SUBLANES
    d_inner = zs.shape[1]
    q = _pick(seqlen, (SSD_CHUNK, 64))
    assert row0 % q == 0 and pdim * 2 == LANES and hg % 2 == 0 and nstate == LANES
    nc = seqlen // q
    blk0 = row0 // q
    gw = hg * pdim
    row_map = lambda b, c: (blk0 + b * nc + c, 0)
    const = lambda b, c: (0, 0)
    seq4 = lambda b, c: (b, 0, 0, 0)
    return pl.pallas_call(
        functools.partial(_ssd_kernel, width=width, q=q, ngroups=ngroups, hg=hg, pdim=pdim, nstate=nstate),
        grid=(nseq, nc),
        in_specs=[pl.BlockSpec((q, dxbc), row_map),
                  pl.BlockSpec((q, d_inner), row_map),
                  pl.BlockSpec((q, LANES), row_map),
                  pl.BlockSpec((1, SUBLANES, dxbc), lambda b, c: (b, 0, 0)),
                  pl.BlockSpec((1, ngroups, nstate, gw), seq4),
                  pl.BlockSpec((width, dxbc), const),
                  pl.BlockSpec((1, dxbc), const),
                  pl.BlockSpec((1, LANES), const),
                  pl.BlockSpec((1, d_inner), const),
                  pl.BlockSpec((1, d_inner), const),
                  pl.BlockSpec((LANES, d_inner), const)],
        out_specs=[pl.BlockSpec((q, d_inner), lambda b, c: (b * nc + c, 0)),
                   pl.BlockSpec((1, ngroups, nstate, gw), seq4)],
        out_shape=[jax.ShapeDtypeStruct((nseq * seqlen, d_inner), BF16),
                   jax.ShapeDtypeStruct((nseq, ngroups, nstate, gw), F32)],
        scratch_shapes=[pltpu.VMEM((q, dxbc), BF16),
                        pltpu.VMEM((SUBLANES, dxbc), F32),
                        pltpu.VMEM((ngroups, q, gw), F32),
                        pltpu.VMEM((ngroups, nstate, gw), F32),
                        pltpu.VMEM((ngroups, hg, q), F32),
                        pltpu.VMEM((ngroups, hg, q), F32),
                        pltpu.VMEM((ngroups, hg, q), F32)],
        compiler_params=_cparams(("arbitrary", "arbitrary")),
        name=name,
    )(xbc, zs, dt, hist, s0, conv_w, conv_b, a_row, dsk_row, ng_row, ex)


def _pad_hist(cache, hp):
    nseq, wm1, c = cache.shape
    return jnp.pad(cache.astype(F32), ((0, 0), (hp - wm1, 0), (0, 0)))


def _tail_rows(rows_all, row0, nseq, seqlen, keep):
    return jnp.stack([lax.slice_in_dim(rows_all, row0 + (b + 1) * seqlen - keep, row0 + (b + 1) * seqlen)
                      for b in range(nseq)]).astype(F32)


def _conv_tails(tail, keep):
    nseq, nw, _, c = tail.shape
    return tail[:, :, SUBLANES - keep:].transpose(0, 2, 1, 3).reshape(nseq, keep, nw * c)


def _layer(xp, xs, segs, cache_s, wts):
    (norm_mix_pre, w_in, b_gate, conv_a_w, conv_a_b, ln_a_g, ln_a_b, w_a_out, conv_b_w, conv_b_b,
     dt_bias, a_log, d_skip, ssd_norm_g, w_b_out, w_o, norm_mix_post, norm_ffn_pre, w_up,
     ffn_conv_w, ffn_conv_b, w_down, norm_ffn_post) = wts
    tp, d = xp.shape
    ts = xs.shape[0]
    (_, nseq_p, len_p), (_, nseq_s, len_s) = segs
    dc = conv_a_w.shape[1]
    wa = conv_a_w.shape[0]
    dxbc = conv_b_w.shape[1]
    wb = conv_b_w.shape[0]
    nheads = a_log.shape[0]
    d_inner = ssd_norm_g.shape[0]
    pdim = d_inner // nheads
    nstate = cache_s[2].shape[-1]
    ngroups = (dxbc - d_inner) // (2 * nstate)
    hg = nheads // ngroups
    gw = hg * pdim
    f2 = ffn_conv_w.shape[1]
    dff = f2 // 2
    wf = ffn_conv_w.shape[0]
    assert nheads <= LANES and min(len_p, len_s) >= max(wa, wb, wf) - 1

    row = lambda v: v.reshape(1, -1).astype(F32)
    tm = _pick(ts, (1024, 512, 256, 128, 64))
    assert tp % tm == 0 and len_p % tm == 0
    n_p = tp // tm
    seq_kw = dict(tm=tm, n_p=n_p, tiles_per_seq=len_p // tm, nseq_p=nseq_p, ls=len_s)

    h = _prenorm(xp, xs, row(norm_mix_pre), tm=tm)

    w_in_t = jnp.swapaxes(w_in, 0, 1)
    o_val, o_gate, o_z, o_xbc = 0, dc, 2 * dc, 2 * dc + d_inner
    o_dt = o_xbc + dxbc
    o_g = o_dt + nheads
    bh = _pick(dc, (512, 256, 128))
    u = _matmul(h, [(w_in_t, bh, o_val // bh), (w_in_t, bh, o_gate // bh)], nj=dc // bh, order="col", tm=tm,
                outs=[(dc, bh, BF16, False)], epilogue=_ep_glu, cast_w=True, w_t=True, name="inproj_glu")
    bz = _pick(d_inner, (1024, 512, 256, 128))
    assert o_z % bz == 0
    zs = _matmul(h, [(w_in_t, bz, o_z // bz)], nj=d_inner // bz, order="col", tm=tm,
                 outs=[(d_inner, bz, BF16, False)], epilogue=_ep_silu, cast_w=True, w_t=True, name="inproj_z")
    bx = _pick(dxbc, (1024, 512, 256, 128))
    assert o_xbc % bx == 0
    xbc = _matmul(h, [(w_in_t, bx, o_xbc // bx)], nj=dxbc // bx, order="col", tm=tm,
                  outs=[(dxbc, bx, BF16, False)], epilogue=_ep_store, cast_w=True, w_t=True, name="inproj_xbc")
    assert o_dt % LANES == 0 and o_dt + LANES <= w_in_t.shape[0]
    dt_b = jnp.pad(dt_bias.astype(F32), (0, LANES - nheads)).reshape(1, LANES)
    dt = _matmul(h, [(w_in_t, LANES, o_dt // LANES)], nj=1, order="col", tm=tm,
                 outs=[(LANES, LANES, F32, False)], epilogue=functools.partial(_ep_dt, nheads=nheads),
                 rows=[(dt_b, LANES, 0)], cast_w=True, w_t=True, name="inproj_dt")
    bg = _pick(2 * d, (1024, 512, 256, 128))
    gates = _matmul(h, [(w_in_t[o_g:].astype(BF16), bg, 0)], nj=2 * d // bg, order="col", tm=tm,
                    outs=[(2 * d, bg, BF16, False)], epilogue=_ep_bias_sigmoid, w_t=True,
                    rows=[(row(b_gate), bg, 0)], name="inproj_gates")

    hp_a = -(-(wa - 1) // SUBLANES) * SUBLANES
    hist_a = [jnp.zeros((nseq_p, hp_a, dc), F32), _pad_hist(cache_s[0], hp_a)]
    a_act = tuple(_conva_call(u, hist_a[si], conv_a_w.astype(F32), row(conv_a_b), row(ln_a_g), row(ln_a_b),
                              row0=row0, nseq=nseq, seqlen=seqlen, name=f"conva_{si}")
                  for si, (row0, nseq, seqlen) in enumerate(segs))
    bd = _pick(d, (1024, 512, 256, 128))
    ya = _matmul(a_act, [(w_a_out.astype(BF16), bd, 0)], nj=d // bd, order="row", tm=tm, n_p=n_p,
                 outs=[(d, bd, BF16, False)], epilogue=_ep_gate, blks=[(gates, bd, 0)], name="proj_a_out")

    a_row = jnp.pad(-jnp.exp(a_log.astype(F32)), (0, LANES - nheads)).reshape(1, LANES)
    dsk_row = jnp.repeat(d_skip.astype(F32), pdim).reshape(1, d_inner)
    ex = (jnp.arange(LANES)[:, None] == (jnp.arange(d_inner)[None, :] // pdim)).astype(BF16)
    s0 = [jnp.zeros((nseq_p, ngroups, nstate, gw), F32),
          cache_s[2].astype(F32).reshape(nseq_s, ngroups, gw, nstate).transpose(0, 1, 3, 2)]
    hist_b = [jnp.zeros((nseq_p, SUBLANES, dxbc), F32), _pad_hist(cache_s[1], SUBLANES)]
    yn, new_states = [], []
    for si, (row0, nseq, seqlen) in enumerate(segs):
        y_si, s_new = _ssd_call(xbc, zs, dt, hist_b[si], s0[si], conv_b_w.astype(F32), row(conv_b_b),
                                a_row, dsk_row, row(ssd_norm_g), ex,
                                row0=row0, nseq=nseq, seqlen=seqlen, ngroups=ngroups, hg=hg,
                                pdim=pdim, nstate=nstate, name=f"ssd_{si}")
        yn.append(y_si)
        new_states.append(s_new.transpose(0, 1, 3, 2).reshape(nseq, nheads, pdim, nstate))
    tm_b = _pick(tm, (512, 256, 128, 64))
    merged = _matmul(tuple(yn), [(w_b_out.astype(BF16), bd, 0)], nj=d // bd, order="row", tm=tm_b,
                     n_p=tp // tm_b, outs=[(d, bd, BF16, False)], epilogue=_ep_gate_add,
                     blks=[(gates, bd, d // bd), (ya, bd, 0)], name="proj_b_out")
    tm_o = _pick(tm, (256, 128, 64))
    x1, h2 = _matmul(merged, [(w_o.astype(BF16), d, 0)], nj=1, order="row", tm=tm_b,
                     outs=[(d, d, F32, False), (d, d, BF16, False)], epilogue=_ep_residual_norm,
                     rows=[(row(norm_mix_post), d, 0), (row(norm_ffn_pre), d, 0)],
                     blks=[((xp, xs), d, 0)], n_p=tp // tm_b, single_buffer_w=True, name="proj_o")

    act, ffn_tail_p, ffn_tail_s = _ffn_up_conv(h2, w_up, ffn_conv_w.astype(F32), row(ffn_conv_b),
                                               _pad_hist(cache_s[3], SUBLANES), name="ffn_up", **seq_kw)
    yp, ys = _matmul(act, [(w_down.astype(BF16), d, 0)], nj=1, order="row", tm=tm_o,
                     outs=[(d, d, F32, True)], epilogue=_ep_residual,
                     rows=[(row(norm_ffn_post), d, 0)], blks=[(x1, d, 0)], n_p=tp // tm_o,
                     single_buffer_w=True, name="ffn_down")

    new_caches = []
    for si, (row0, nseq, seqlen) in enumerate(segs):
        new_caches.append((_tail_rows(u, row0, nseq, seqlen, wa - 1),
                           _tail_rows(xbc, row0, nseq, seqlen, wb - 1),
                           new_states[si],
                           _conv_tails((ffn_tail_p, ffn_tail_s)[si], wf - 1)))
    return yp, ys, new_caches


def kernel(x_prompt, x_sample, cache_conv_a, cache_conv_b, state_ssd, cache_ffn_conv, norm_mix_pre, w_in, b_gate, conv_a_w, conv_a_b, ln_a_g, ln_a_b, w_a_out, conv_b_w, conv_b_b, dt_bias, a_log, d_skip, ssd_norm_g, w_b_out, w_o, norm_mix_post, norm_ffn_pre, w_up, ffn_conv_w, ffn_conv_b, w_down, norm_ffn_post):
    weights = (norm_mix_pre, w_in, b_gate, conv_a_w, conv_a_b, ln_a_g, ln_a_b, w_a_out, conv_b_w,
               conv_b_b, dt_bias, a_log, d_skip, ssd_norm_g, w_b_out, w_o, norm_mix_post,
               norm_ffn_pre, w_up, ffn_conv_w, ffn_conv_b, w_down, norm_ffn_post)
    depth = w_in.shape[0]
    bp, lp, d = x_prompt.shape
    bs, ls, _ = x_sample.shape
    tp = bp * lp
    segs = ((0, bp, lp), (tp, bs, ls))
    xp = x_prompt.reshape(tp, d)
    xs = x_sample.reshape(bs * ls, d)
    dt_in = x_prompt.dtype
    outs_p = ([], [], [], [])
    outs_s = ([], [], [], [])
    for layer in range(depth):
        wl = tuple(w[layer] for w in weights)
        cache_s = (cache_conv_a[layer], cache_conv_b[layer], state_ssd[layer], cache_ffn_conv[layer])
        xp, xs, (new_p, new_s) = _layer(xp, xs, segs, cache_s, wl)
        for lst, v in zip(outs_p, new_p):
            lst.append(v.astype(dt_in))
        for lst, v in zip(outs_s, new_s):
            lst.append(v.astype(dt_in))
    return (xp.reshape(bp, lp, d), xs.reshape(bs, ls, d),
            jnp.stack(outs_p[0]), jnp.stack(outs_p[1]), jnp.stack(outs_p[2]), jnp.stack(outs_p[3]),
            jnp.stack(outs_s[0]), jnp.stack(outs_s[1]), jnp.stack(outs_s[2]), jnp.stack(outs_s[3]))
```

```python
import functools

import jax
import jax.numpy as jnp
from jax import lax
from jax.experimental import pallas as pl
from jax.experimental.pallas import tpu as pltpu

F32 = jnp.float32
BF16 = jnp.bfloat16
EPS = 1e-6
LANES = 128
SUBLANES = 8
VMEM_LIMIT = 56 * 1024 * 1024
SSD_CHUNK = 128
CONV_LANES = 512
NEG_BIG = -1e30
LOG2E = 1.4426950408889634
NN_DIMS = (((1,), (0,)), ((), ()))
NT_DIMS = (((1,), (1,)), ((), ()))


def _pick(n, cands):
    for c in cands:
        if n % c == 0:
            return c
    raise ValueError(f"no tile in {cands} divides {n}")


def _cparams(sem):
    return pltpu.CompilerParams(dimension_semantics=sem, vmem_limit_bytes=VMEM_LIMIT)


def _sigmoid(x):
    return 1.0 / (1.0 + jnp.exp(-x))


def _silu(x):
    return x * _sigmoid(x)


def _softplus(x):
    return jnp.maximum(x, 0.0) + jnp.log1p(jnp.exp(-jnp.abs(x)))


def _rms_scale(x, eps):
    return x * lax.rsqrt(jnp.mean(x * x, axis=-1, keepdims=True) + eps)


def _prenorm_kernel(xp_ref, xs_ref, g_ref, o_ref, *, n_p):
    i = pl.program_id(0)

    def run(x_ref):
        o_ref[...] = (_rms_scale(x_ref[...].astype(F32), EPS) * g_ref[...]).astype(o_ref.dtype)

    pl.when(i < n_p)(functools.partial(run, xp_ref))
    pl.when(i >= n_p)(functools.partial(run, xs_ref))


def _prenorm(xp, xs, gain, *, tm):
    tp, d = xp.shape
    ts = xs.shape[0]
    n_p = tp // tm
    return pl.pallas_call(
        functools.partial(_prenorm_kernel, n_p=n_p),
        grid=((tp + ts) // tm,),
        in_specs=[pl.BlockSpec((tm, d), lambda i: (jnp.minimum(i, n_p - 1), 0)),
                  pl.BlockSpec((tm, d), lambda i: (jnp.maximum(i - n_p, 0), 0)),
                  pl.BlockSpec((1, d), lambda i: (0, 0))],
        out_specs=pl.BlockSpec((tm, d), lambda i: (i, 0)),
        out_shape=jax.ShapeDtypeStruct((tp + ts, d), BF16),
        compiler_params=_cparams(("arbitrary",)),
        name="prenorm",
    )(xp, xs, gain)


def _mm_kernel(*refs, order, n_w, cast_w, w_t, lhs_dual, n_rows, blk_dual, out_dual, n_p, epilogue):
    it = iter(refs)
    lhs_refs = tuple(next(it) for _ in range(2 if lhs_dual else 1))
    w_refs = [next(it) for _ in range(n_w)]
    row_refs = [next(it) for _ in range(n_rows)]
    blk_refs = [tuple(next(it) for _ in range(2 if d else 1)) for d in blk_dual]
    out_refs = [tuple(next(it) for _ in range(2 if d else 1)) for d in out_dual]
    i = pl.program_id(1 if order == "col" else 0)
    if cast_w:
        wbuf = next(it)

        @pl.when(i == 0)
        def _():
            c0 = 0
            for wr in w_refs:
                if w_t:
                    wbuf[c0:c0 + wr.shape[0], :] = wr[...].astype(BF16)
                    c0 += wr.shape[0]
                else:
                    wbuf[:, c0:c0 + wr.shape[1]] = wr[...].astype(BF16)
                    c0 += wr.shape[1]

        w_ref = wbuf
    else:
        w_ref = w_refs[0]

    def run(sel):
        pick = lambda pair: pair[sel if len(pair) > 1 else 0]
        r = lax.dot_general(pick(lhs_refs)[...], w_ref[...], NT_DIMS if w_t else NN_DIMS,
                            preferred_element_type=F32)
        epilogue(r, row_refs, [pick(b) for b in blk_refs], [pick(o) for o in out_refs])

    if lhs_dual or any(blk_dual) or any(out_dual):
        pl.when(i < n_p)(functools.partial(run, 0))
        pl.when(i >= n_p)(functools.partial(run, 1))
    else:
        run(0)


def _matmul(lhs, ws, *, nj, order, tm, outs, epilogue, name, rows=(), blks=(), cast_w=False,
            w_t=False, n_p=None, single_buffer_w=False):
    lhs_dual = isinstance(lhs, tuple)
    t = lhs[0].shape[0] + lhs[1].shape[0] if lhs_dual else lhs.shape[0]
    k = lhs[0].shape[1] if lhs_dual else lhs.shape[1]
    ni = t // tm
    if order == "col":
        grid = (nj, ni)
        spec = lambda shape, fn, **kw: pl.BlockSpec(shape, lambda a, b: fn(b, a), **kw)
    else:
        grid = (ni, nj)
        spec = lambda shape, fn, **kw: pl.BlockSpec(shape, lambda a, b: fn(a, b), **kw)
    assert not cast_w or order == "col" or nj == 1
    prompt_row = lambda i: jnp.minimum(i, n_p - 1)
    decode_row = lambda i: jnp.maximum(i - n_p, 0)

    if lhs_dual:
        in_specs = [spec((tm, k), lambda i, j: (prompt_row(i), 0)),
                    spec((tm, k), lambda i, j: (decode_row(i), 0))]
        args = list(lhs)
    else:
        in_specs = [spec((tm, k), lambda i, j: (i, 0))]
        args = [lhs]
    w_kw = {"pipeline_mode": pl.Buffered(1)} if single_buffer_w else {}
    bn = 0
    for arr, bw, off in ws:
        if w_t:
            in_specs.append(spec((bw, k), lambda i, j, off=off: (j + off, 0), **w_kw))
        else:
            in_specs.append(spec((k, bw), lambda i, j, off=off: (0, j + off), **w_kw))
        args.append(arr)
        bn += bw
    for arr, bw, off in rows:
        in_specs.append(spec((1, bw), lambda i, j, off=off: (0, j + off)))
        args.append(arr)
    for arr, bw, off in blks:
        if isinstance(arr, tuple):
            in_specs.append(spec((tm, bw), lambda i, j, off=off: (prompt_row(i), j + off)))
            in_specs.append(spec((tm, bw), lambda i, j, off=off: (decode_row(i), j + off)))
            args.extend(arr)
        else:
            in_specs.append(spec((tm, bw), lambda i, j, off=off: (i, j + off)))
            args.append(arr)
    out_specs, out_shape = [], []
    for cols, bw, dtype, dual in outs:
        if dual:
            out_specs.append(spec((tm, bw), lambda i, j: (prompt_row(i), j)))
            out_specs.append(spec((tm, bw), lambda i, j: (decode_row(i), j)))
            out_shape.append(jax.ShapeDtypeStruct((n_p * tm, cols), dtype))
            out_shape.append(jax.ShapeDtypeStruct(((ni - n_p) * tm, cols), dtype))
        else:
            out_specs.append(spec((tm, bw), lambda i, j: (i, j)))
            out_shape.append(jax.ShapeDtypeStruct((t, cols), dtype))
    scratch = [pltpu.VMEM((bn, k) if w_t else (k, bn), BF16)] if cast_w else []
    res = pl.pallas_call(
        functools.partial(_mm_kernel, order=order, n_w=len(ws), cast_w=cast_w, w_t=w_t,
                          lhs_dual=lhs_dual, n_rows=len(rows),
                          blk_dual=[isinstance(b[0], tuple) for b in blks],
                          out_dual=[o[3] for o in outs], n_p=n_p, epilogue=epilogue),
        grid=grid,
        in_specs=in_specs,
        out_specs=out_specs,
        out_shape=out_shape,
        scratch_shapes=scratch,
        compiler_params=_cparams(("arbitrary", "arbitrary")),
        name=name,
    )(*args)
    return res[0] if len(res) == 1 else res


def _ep_glu(r, rows, blks, outs):
    half = r.shape[1] // 2
    outs[0][...] = (r[:, :half] * _sigmoid(r[:, half:])).astype(outs[0].dtype)


def _ep_silu(r, rows, blks, outs):
    outs[0][...] = _silu(r).astype(outs[0].dtype)


def _ep_store(r, rows, blks, outs):
    outs[0][...] = r.astype(outs[0].dtype)


def _ep_bias_sigmoid(r, rows, blks, outs):
    outs[0][...] = _sigmoid(r + rows[0][...]).astype(outs[0].dtype)


def _ep_dt(r, rows, blks, outs, *, nheads):
    lane = lax.broadcasted_iota(jnp.int32, r.shape, 1)
    outs[0][...] = jnp.where(lane < nheads, _softplus(r + rows[0][...]), 0.0).astype(outs[0].dtype)


def _ep_gate(r, rows, blks, outs):
    outs[0][...] = (r * blks[0][...].astype(F32)).astype(outs[0].dtype)


def _ep_gate_add(r, rows, blks, outs):
    outs[0][...] = (r * blks[0][...].astype(F32) + blks[1][...].astype(F32)).astype(outs[0].dtype)


def _ep_residual_norm(r, rows, blks, outs):
    x1 = blks[0][...] + _rms_scale(r, EPS) * rows[0][...]
    outs[0][...] = x1
    outs[1][...] = (_rms_scale(x1, EPS) * rows[1][...]).astype(outs[1].dtype)


def _ep_residual(r, rows, blks, outs):
    outs[0][...] = blks[0][...] + _rms_scale(r, EPS) * rows[0][...]


def _ffnup_kernel(h_ref, wg_ref, wv_ref, cwg_ref, cwv_ref, cbg_ref, cbv_ref, hg_ref, hv_ref,
                  act_ref, tailp_ref, tails_ref, wbuf, cbuf, sbuf,
                  *, width, tm, bw, sm, n_p, tiles_per_seq, ls):
    hp = SUBLANES
    i = pl.program_id(1)

    @pl.when(i == 0)
    def _():
        wbuf[:, 0:bw] = wg_ref[...].astype(BF16)
        wbuf[:, bw:2 * bw] = wv_ref[...].astype(BF16)

    w = wbuf[...]

    def conv_act(buf, r0, n):
        def conv(c0, cw_ref, cb_ref):
            acc = jnp.zeros((n, bw), F32) + cb_ref[...]
            for k in range(width):
                off = r0 - (width - 1) + k
                acc = acc + buf[off:off + n, c0:c0 + bw] * cw_ref[k:k + 1, :]
            return acc
        return (jax.nn.gelu(conv(0, cwg_ref, cbg_ref)) * conv(bw, cwv_ref, cbv_ref)).astype(act_ref.dtype)

    @pl.when(i < n_p)
    def _():
        @pl.when(i % tiles_per_seq == 0)
        def _():
            cbuf[0:hp, :] = jnp.zeros((hp, 2 * bw), F32)

        for s in range(tm // sm):
            cbuf[hp + s * sm:hp + (s + 1) * sm, :] = jnp.dot(
                h_ref[s * sm:(s + 1) * sm, :], w, preferred_element_type=F32)
            act_ref[s * sm:(s + 1) * sm, :] = conv_act(cbuf, hp + s * sm, sm)

        @pl.when(i % tiles_per_seq == tiles_per_seq - 1)
        def _():
            tailp_ref[0, 0] = cbuf[tm:tm + hp, 0:bw]
            tailp_ref[0, 1] = cbuf[tm:tm + hp, bw:2 * bw]

        cbuf[0:hp, :] = cbuf[tm:tm + hp, :]

    @pl.when(i >= n_p)
    def _():
        for s in range(tm // sm):
            r = jnp.dot(h_ref[s * sm:(s + 1) * sm, :], w, preferred_element_type=F32)
            for q in range(sm // ls):
                seq = s * (sm // ls) + q
                sbuf[0:hp, 0:bw] = hg_ref[seq]
                sbuf[0:hp, bw:2 * bw] = hv_ref[seq]
                rq = r[q * ls:(q + 1) * ls, :]
                sbuf[hp:hp + ls, :] = rq
                act_ref[s * sm + q * ls:s * sm + (q + 1) * ls, :] = conv_act(sbuf, hp, ls)
                tails_ref[seq, 0] = rq[ls - hp:ls, 0:bw]
                tails_ref[seq, 1] = rq[ls - hp:ls, bw:2 * bw]


def _ffn_up_conv(h, w_up, conv_w, conv_b, hist_s, *, tm, n_p, tiles_per_seq, nseq_p, ls, name):
    t, k = h.shape
    f2 = w_up.shape[1]
    dff = f2 // 2
    width = conv_w.shape[0]
    bw = _pick(dff, (512, 256, 128))
    nj = dff // bw
    ni = t // tm
    sm = _pick(tm, (256, 128, 64))
    spt = tm // ls
    nseq_s = hist_s.shape[0]
    assert tm % ls == 0 and sm % ls == 0 and ls >= SUBLANES and width - 1 <= SUBLANES
    decode_tile = lambda i: jnp.maximum(i - n_p, 0)
    prompt_seq = lambda i: jnp.minimum(i, n_p - 1) // tiles_per_seq
    in_specs = [pl.BlockSpec((tm, k), lambda j, i: (i, 0)),
                pl.BlockSpec((k, bw), lambda j, i: (0, j)),
                pl.BlockSpec((k, bw), lambda j, i: (0, nj + j)),
                pl.BlockSpec((width, bw), lambda j, i: (0, j)),
                pl.BlockSpec((width, bw), lambda j, i: (0, nj + j)),
                pl.BlockSpec((1, bw), lambda j, i: (0, j)),
                pl.BlockSpec((1, bw), lambda j, i: (0, nj + j)),
                pl.BlockSpec((spt, SUBLANES, bw), lambda j, i: (decode_tile(i), 0, j)),
                pl.BlockSpec((spt, SUBLANES, bw), lambda j, i: (decode_tile(i), 0, nj + j))]
    out_specs = [pl.BlockSpec((tm, bw), lambda j, i: (i, j)),
                 pl.BlockSpec((1, 2, SUBLANES, bw), lambda j, i: (prompt_seq(i), 0, 0, j)),
                 pl.BlockSpec((spt, 2, SUBLANES, bw), lambda j, i: (decode_tile(i), 0, 0, j))]
    out_shape = [jax.ShapeDtypeStruct((t, dff), BF16),
                 jax.ShapeDtypeStruct((nseq_p, 2, SUBLANES, dff), F32),
                 jax.ShapeDtypeStruct((nseq_s, 2, SUBLANES, dff), F32)]
    return pl.pallas_call(
        functools.partial(_ffnup_kernel, width=width, tm=tm, bw=bw, sm=sm, n_p=n_p,
                          tiles_per_seq=tiles_per_seq, ls=ls),
        grid=(nj, ni),
        in_specs=in_specs,
        out_specs=out_specs,
        out_shape=out_shape,
        scratch_shapes=[pltpu.VMEM((k, 2 * bw), BF16),
                        pltpu.VMEM((SUBLANES + tm, 2 * bw), F32),
                        pltpu.VMEM((SUBLANES + ls, 2 * bw), F32)],
        compiler_params=_cparams(("arbitrary", "arbitrary")),
        name=name,
    )(h, w_up, w_up, conv_w, conv_w, conv_b, conv_b, hist_s, hist_s)


def _conva_kernel(u_ref, hist_ref, w_ref, cb_ref, lg_ref, lb_ref, o_ref, bufp_ref, cv_ref,
                  *, width, hp, ta, rs, rc, lc):
    c = u_ref.shape[1]
    t = pl.program_id(1)
    n_ext = hp + ta

    @pl.when(t == 0)
    def _():
        bufp_ref[0, 0:hp, :] = hist_ref[0]
        bufp_ref[0, n_ext:n_ext + SUBLANES, :] = jnp.zeros((SUBLANES, c), F32)

    bufp_ref[0, hp:hp + ta, :] = u_ref[...].astype(F32)

    def shift_body(r, carry):
        s0 = pl.multiple_of(r * rs, rs)
        win = bufp_ref[0, pl.ds(s0, rs + SUBLANES), :]
        for p in range(1, SUBLANES):
            bufp_ref[p, pl.ds(s0, rs), :] = win[p:p + rs, :]
        return carry

    lax.fori_loop(0, n_ext // rs, shift_body, 0)

    def conv_body(r, carry):
        r0 = pl.multiple_of(r * rc, rc)
        for l0 in range(0, c, lc):
            acc = jnp.zeros((rc, lc), F32)
            for k in range(width):
                off = hp - (width - 1) + k
                q8, p = (off // SUBLANES) * SUBLANES, off % SUBLANES
                xs = bufp_ref[p, pl.ds(r0 + q8, rc), l0:l0 + lc]
                acc = acc + xs * w_ref[k:k + 1, l0:l0 + lc]
            cv_ref[:, l0:l0 + lc] = acc
        v = cv_ref[...] + cb_ref[...]
        mu = jnp.mean(v, axis=-1, keepdims=True)
        vc = v - mu
        var = jnp.mean(vc * vc, axis=-1, keepdims=True)
        y = vc * lax.rsqrt(var + EPS) * lg_ref[...] + lb_ref[...]
        o_ref[pl.ds(r0, rc), :] = _silu(y).astype(o_ref.dtype)
        return carry

    lax.fori_loop(0, ta // rc, conv_body, 0)
    bufp_ref[0, 0:hp, :] = bufp_ref[0, ta:ta + hp, :]


def _conva_call(u, hist, w, cb, lg, lb, *, row0, nseq, seqlen, name):
    c = u.shape[1]
    width = w.shape[0]
    hp = hist.shape[1]
    ta = _pick(seqlen, (256, 128, 64))
    rs = 32
    rc, lc = 32, 128
    assert row0 % ta == 0 and (hp + ta) % rs == 0 and ta % rc == 0 and hp >= width - 1
    nt = seqlen // ta
    blk0 = row0 // ta
    const = lambda b, t: (0, 0)
    return pl.pallas_call(
        functools.partial(_conva_kernel, width=width, hp=hp, ta=ta, rs=rs, rc=rc, lc=lc),
        grid=(nseq, nt),
        in_specs=[pl.BlockSpec((ta, c), lambda b, t: (blk0 + b * nt + t, 0)),
                  pl.BlockSpec((1, hp, c), lambda b, t: (b, 0, 0)),
                  pl.BlockSpec((width, c), const),
                  pl.BlockSpec((1, c), const),
                  pl.BlockSpec((1, c), const),
                  pl.BlockSpec((1, c), const)],
        out_specs=pl.BlockSpec((ta, c), lambda b, t: (b * nt + t, 0)),
        out_shape=jax.ShapeDtypeStruct((nseq * seqlen, c), BF16),
        scratch_shapes=[pltpu.VMEM((SUBLANES, hp + ta + SUBLANES, c), F32), pltpu.VMEM((rc, c), F32)],
        compiler_params=_cparams(("arbitrary", "arbitrary")),
        name=name,
    )(u, hist, w, cb, lg, lb)


def _ssd_kernel(raw_ref, zs_ref, dt_ref, hist_ref, s0_ref, cw_ref, cb_ref, a_ref, dsk_ref, ng_ref, ex_ref,
                y_ref, sout_ref, xbc_ref, hbuf_ref, y3_ref, st_ref, cumt_ref, cml_ref, wgt_ref,
                *, width, q, ngroups, hg, pdim, nstate):
    gw = hg * pdim
    d_inner = ngroups * gw
    c = pl.program_id(1)
    nchunks = pl.num_programs(1)

    @pl.when(c == 0)
    def _():
        st_ref[...] = s0_ref[0]
        hbuf_ref[...] = hist_ref[0]

    for l0 in range(0, raw_ref.shape[1], CONV_LANES):
        cols = slice(l0, l0 + CONV_LANES)
        raw = raw_ref[:, cols].astype(F32)
        e = jnp.concatenate([hbuf_ref[:, cols], raw], axis=0)
        acc = e * cw_ref[0:1, cols]
        for k in range(1, width):
            acc = e * cw_ref[k:k + 1, cols] + pltpu.roll(acc, 1, axis=0)
        v = acc[SUBLANES:, :] + cb_ref[:, cols]
        xbc_ref[:, cols] = (0.5 * v * (1.0 + jnp.tanh(0.5 * v))).astype(xbc_ref.dtype)
        hbuf_ref[:, cols] = raw[q - SUBLANES:q, :]

    dt = dt_ref[...]
    da = dt * a_ref[...]
    ii = lax.broadcasted_iota(jnp.int32, (q, q), 0)
    jj = lax.broadcasted_iota(jnp.int32, (q, q), 1)
    causal = ii >= jj
    tri = jnp.where(causal, 1.0, 0.0).astype(BF16)
    da_hi = da.astype(BF16)
    da_lo = (da - da_hi.astype(F32)).astype(BF16)
    cum = (jnp.dot(tri, da_hi, preferred_element_type=F32)
           + jnp.dot(tri, da_lo, preferred_element_type=F32))
    cum_t = cum.T
    dt_t = dt.T
    last_t = cum_t[:, q - 1:q]
    nh = ngroups * hg
    c2_t = cum_t * LOG2E
    cumt_ref[...] = c2_t[0:nh, :].reshape(ngroups, hg, q)
    cml_ref[...] = (c2_t - jnp.log2(dt_t))[0:nh, :].reshape(ngroups, hg, q)
    wgt_ref[...] = (dt_t * jnp.exp(last_t - cum_t))[0:nh, :].reshape(ngroups, hg, q)

    e_last = jnp.exp(cum[q - SUBLANES:q, :])
    e_hi = e_last.astype(BF16)
    e_lo = (e_last - e_hi.astype(F32)).astype(BF16)
    dec = (jnp.dot(e_hi, ex_ref[...], preferred_element_type=F32)
           + jnp.dot(e_lo, ex_ref[...], preferred_element_type=F32))
    dec_row = dec[SUBLANES - 1:SUBLANES, :]

    lane = lax.broadcasted_iota(jnp.int32, (1, LANES), 1)
    lo_mask = lane < pdim
    b0 = d_inner
    c0 = d_inner + ngroups * nstate

    for g in range(ngroups):
        bg = xbc_ref[:, b0 + g * nstate:b0 + (g + 1) * nstate]
        cg = xbc_ref[:, c0 + g * nstate:c0 + (g + 1) * nstate]
        scores = lax.dot_general(cg, bg, NT_DIMS, preferred_element_type=F32)
        bg_t = bg.astype(F32).T
        cg32 = cg.astype(F32)
        cum_g = cumt_ref[g]
        cml_g = cml_ref[g]
        wg_g = wgt_ref[g]
        for hp2 in range(hg // 2):
            l0 = g * gw + hp2 * LANES
            xpair = xbc_ref[:, l0:l0 + LANES]
            spair = st_ref[g, :, hp2 * LANES:(hp2 + 1) * LANES]
            lhs_parts, rhs_parts, bw_parts, x_parts = [], [], [], []
            for part in range(2):
                hh = 2 * hp2 + part
                mask = lo_mask if part == 0 else jnp.logical_not(lo_mask)
                x_h = jnp.where(mask, xpair, jnp.zeros_like(xpair))
                s_h = jnp.where(mask, spair, 0.0).astype(BF16)
                row = cum_g[hh:hh + 1, :]
                col = jnp.broadcast_to(row, (LANES, q)).T
                seg = col[:, 0:q] - cml_g[hh:hh + 1, :]
                m_h = scores * jnp.exp2(jnp.where(causal, seg, NEG_BIG))
                cs_h = cg32 * jnp.exp2(col[:, 0:nstate])
                lhs_parts += [m_h.astype(BF16), cs_h.astype(BF16)]
                rhs_parts += [x_h, s_h]
                bw_parts.append((bg_t * wg_g[hh:hh + 1, :]).astype(BF16))
                x_parts.append(x_h)
            ypair = jnp.dot(jnp.concatenate(lhs_parts, axis=1), jnp.concatenate(rhs_parts, axis=0),
                            preferred_element_type=F32)
            upd = jnp.dot(jnp.concatenate(bw_parts, axis=1), jnp.concatenate(x_parts, axis=0),
                          preferred_element_type=F32)
            y3_ref[g, :, hp2 * LANES:(hp2 + 1) * LANES] = ypair
            st_ref[g, :, hp2 * LANES:(hp2 + 1) * LANES] = spair * dec_row[:, l0:l0 + LANES] + upd

    ssq = jnp.zeros((q, 1), F32)
    for g in range(ngroups):
        cols = slice(g * gw, (g + 1) * gw)
        yg = (y3_ref[g] + dsk_ref[:, cols] * xbc_ref[:, cols].astype(F32)) * zs_ref[:, cols].astype(F32)
        y3_ref[g] = yg
        ssq = ssq + jnp.sum(yg * yg, axis=-1, keepdims=True)
    inv = lax.rsqrt(ssq / d_inner + EPS)
    for g in range(ngroups):
        cols = slice(g * gw, (g + 1) * gw)
        y_ref[:, cols] = (y3_ref[g] * inv * ng_ref[:, cols]).astype(y_ref.dtype)

    @pl.when(c == nchunks - 1)
    def _():
        sout_ref[0] = st_ref[...]


def _ssd_call(xbc, zs, dt, hist, s0, conv_w, conv_b, a_row, dsk_row, ng_row, ex, *,
              row0, nseq, seqlen, ngroups, hg, pdim, nstate, name):
    dxbc = xbc.shape[1]
    width = conv_w.shape[0]
    assert dxbc % CONV_LANES == 0 and width - 1 <= SUBLANES
    d_inner = zs.shape[1]
    q = _pick(seqlen, (SSD_CHUNK, 64))
    assert row0 % q == 0 and pdim * 2 == LANES and hg % 2 == 0 and nstate == LANES
    nc = seqlen // q
    blk0 = row0 // q
    gw = hg * pdim
    row_map = lambda b, c: (blk0 + b * nc + c, 0)
    const = lambda b, c: (0, 0)
    seq4 = lambda b, c: (b, 0, 0, 0)
    return pl.pallas_call(
        functools.partial(_ssd_kernel, width=width, q=q, ngroups=ngroups, hg=hg, pdim=pdim, nstate=nstate),
        grid=(nseq, nc),
        in_specs=[pl.BlockSpec((q, dxbc), row_map),
                  pl.BlockSpec((q, d_inner), row_map),
                  pl.BlockSpec((q, LANES), row_map),
                  pl.BlockSpec((1, SUBLANES, dxbc), lambda b, c: (b, 0, 0)),
                  pl.BlockSpec((1, ngroups, nstate, gw), seq4),
                  pl.BlockSpec((width, dxbc), const),
                  pl.BlockSpec((1, dxbc), const),
                  pl.BlockSpec((1, LANES), const),
                  pl.BlockSpec((1, d_inner), const),
                  pl.BlockSpec((1, d_inner), const),
                  pl.BlockSpec((LANES, d_inner), const)],
        out_specs=[pl.BlockSpec((q, d_inner), lambda b, c: (b * nc + c, 0)),
                   pl.BlockSpec((1, ngroups, nstate, gw), seq4)],
        out_shape=[jax.ShapeDtypeStruct((nseq * seqlen, d_inner), BF16),
                   jax.ShapeDtypeStruct((nseq, ngroups, nstate, gw), F32)],
        scratch_shapes=[pltpu.VMEM((q, dxbc), BF16),
                        pltpu.VMEM((SUBLANES, dxbc), F32),
                        pltpu.VMEM((ngroups, q, gw), F32),
                        pltpu.VMEM((ngroups, nstate, gw), F32),
                        pltpu.VMEM((ngroups, hg, q), F32),
                        pltpu.VMEM((ngroups, hg, q), F32),
                        pltpu.VMEM((ngroups, hg, q), F32)],
        compiler_params=_cparams(("arbitrary", "arbitrary")),
        name=name,
    )(xbc, zs, dt, hist, s0, conv_w, conv_b, a_row, dsk_row, ng_row, ex)


def _pad_hist(cache, hp):
    nseq, wm1, c = cache.shape
    return jnp.pad(cache.astype(F32), ((0, 0), (hp - wm1, 0), (0, 0)))


def _tail_rows(rows_all, row0, nseq, seqlen, keep):
    return jnp.stack([lax.slice_in_dim(rows_all, row0 + (b + 1) * seqlen - keep, row0 + (b + 1) * seqlen)
                      for b in range(nseq)]).astype(F32)


def _conv_tails(tail, keep):
    nseq, nw, _, c = tail.shape
    return tail[:, :, SUBLANES - keep:].transpose(0, 2, 1, 3).reshape(nseq, keep, nw * c)


def _layer(xp, xs, segs, cache_s, wts):
    (norm_mix_pre, w_in, b_gate, conv_a_w, conv_a_b, ln_a_g, ln_a_b, w_a_out, conv_b_w, conv_b_b,
     dt_bias, a_log, d_skip, ssd_norm_g, w_b_out, w_o, norm_mix_post, norm_ffn_pre, w_up,
     ffn_conv_w, ffn_conv_b, w_down, norm_ffn_post) = wts
    tp, d = xp.shape
    ts = xs.shape[0]
    (_, nseq_p, len_p), (_, nseq_s, len_s) = segs
    dc = conv_a_w.shape[1]
    wa = conv_a_w.shape[0]
    dxbc = conv_b_w.shape[1]
    wb = conv_b_w.shape[0]
    nheads = a_log.shape[0]
    d_inner = ssd_norm_g.shape[0]
    pdim = d_inner // nheads
    nstate = cache_s[2].shape[-1]
    ngroups = (dxbc - d_inner) // (2 * nstate)
    hg = nheads // ngroups
    gw = hg * pdim
    f2 = ffn_conv_w.shape[1]
    dff = f2 // 2
    wf = ffn_conv_w.shape[0]
    assert nheads <= LANES and min(len_p, len_s) >= max(wa, wb, wf) - 1

    row = lambda v: v.reshape(1, -1).astype(F32)
    tm = _pick(ts, (1024, 512, 256, 128, 64))
    assert tp % tm == 0 and len_p % tm == 0
    n_p = tp // tm
    seq_kw = dict(tm=tm, n_p=n_p, tiles_per_seq=len_p // tm, nseq_p=nseq_p, ls=len_s)

    h = _prenorm(xp, xs, row(norm_mix_pre), tm=tm)

    w_in_t = jnp.swapaxes(w_in, 0, 1)
    o_val, o_gate, o_z, o_xbc = 0, dc, 2 * dc, 2 * dc + d_inner
    o_dt = o_xbc + dxbc
    o_g = o_dt + nheads
    bh = _pick(dc, (512, 256, 128))
    u = _matmul(h, [(w_in_t, bh, o_val // bh), (w_in_t, bh, o_gate // bh)], nj=dc // bh, order="col", tm=tm,
                outs=[(dc, bh, BF16, False)], epilogue=_ep_glu, cast_w=True, w_t=True, name="inproj_glu")
    bz = _pick(d_inner, (1024, 512, 256, 128))
    assert o_z % bz == 0
    zs = _matmul(h, [(w_in_t, bz, o_z // bz)], nj=d_inner // bz, order="col", tm=tm,
                 outs=[(d_inner, bz, BF16, False)], epilogue=_ep_silu, cast_w=True, w_t=True, name="inproj_z")
    bx = _pick(dxbc, (1024, 512, 256, 128))
    assert o_xbc % bx == 0
    xbc = _matmul(h, [(w_in_t, bx, o_xbc // bx)], nj=dxbc // bx, order="col", tm=tm,
                  outs=[(dxbc, bx, BF16, False)], epilogue=_ep_store, cast_w=True, w_t=True, name="inproj_xbc")
    assert o_dt % LANES == 0 and o_dt + LANES <= w_in_t.shape[0]
    dt_b = jnp.pad(dt_bias.astype(F32), (0, LANES - nheads)).reshape(1, LANES)
    dt = _matmul(h, [(w_in_t, LANES, o_dt // LANES)], nj=1, order="col", tm=tm,
                 outs=[(LANES, LANES, F32, False)], epilogue=functools.partial(_ep_dt, nheads=nheads),
                 rows=[(dt_b, LANES, 0)], cast_w=True, w_t=True, name="inproj_dt")
    bg = _pick(2 * d, (1024, 512, 256, 128))
    gates = _matmul(h, [(w_in_t[o_g:].astype(BF16), bg, 0)], nj=2 * d // bg, order="col", tm=tm,
                    outs=[(2 * d, bg, BF16, False)], epilogue=_ep_bias_sigmoid, w_t=True,
                    rows=[(row(b_gate), bg, 0)], name="inproj_gates")

    hp_a = -(-(wa - 1) // SUBLANES) * SUBLANES
    hist_a = [jnp.zeros((nseq_p, hp_a, dc), F32), _pad_hist(cache_s[0], hp_a)]
    a_act = tuple(_conva_call(u, hist_a[si], conv_a_w.astype(F32), row(conv_a_b), row(ln_a_g), row(ln_a_b),
                              row0=row0, nseq=nseq, seqlen=seqlen, name=f"conva_{si}")
                  for si, (row0, nseq, seqlen) in enumerate(segs))
    bd = _pick(d, (1024, 512, 256, 128))
    ya = _matmul(a_act, [(w_a_out.astype(BF16), bd, 0)], nj=d // bd, order="row", tm=tm, n_p=n_p,
                 outs=[(d, bd, BF16, False)], epilogue=_ep_gate, blks=[(gates, bd, 0)], name="proj_a_out")

    a_row = jnp.pad(-jnp.exp(a_log.astype(F32)), (0, LANES - nheads)).reshape(1, LANES)
    dsk_row = jnp.repeat(d_skip.astype(F32), pdim).reshape(1, d_inner)
    ex = (jnp.arange(LANES)[:, None] == (jnp.arange(d_inner)[None, :] // pdim)).astype(BF16)
    s0 = [jnp.zeros((nseq_p, ngroups, nstate, gw), F32),
          cache_s[2].astype(F32).reshape(nseq_s, ngroups, gw, nstate).transpose(0, 1, 3, 2)]
    hist_b = [jnp.zeros((nseq_p, SUBLANES, dxbc), F32), _pad_hist(cache_s[1], SUBLANES)]
    yn, new_states = [], []
    for si, (row0, nseq, seqlen) in enumerate(segs):
        y_si, s_new = _ssd_call(xbc, zs, dt, hist_b[si], s0[si], conv_b_w.astype(F32), row(conv_b_b),
                                a_row, dsk_row, row(ssd_norm_g), ex,
                                row0=row0, nseq=nseq, seqlen=seqlen, ngroups=ngroups, hg=hg,
                                pdim=pdim, nstate=nstate, name=f"ssd_{si}")
        yn.append(y_si)
        new_states.append(s_new.transpose(0, 1, 3, 2).reshape(nseq, nheads, pdim, nstate))
    tm_b = _pick(tm, (512, 256, 128, 64))
    merged = _matmul(tuple(yn), [(w_b_out.astype(BF16), bd, 0)], nj=d // bd, order="row", tm=tm_b,
                     n_p=tp // tm_b, outs=[(d, bd, BF16, False)], epilogue=_ep_gate_add,
                     blks=[(gates, bd, d // bd), (ya, bd, 0)], name="proj_b_out")
    tm_o = _pick(tm, (256, 128, 64))
    x1, h2 = _matmul(merged, [(w_o.astype(BF16), d, 0)], nj=1, order="row", tm=tm_b,
                     outs=[(d, d, F32, False), (d, d, BF16, False)], epilogue=_ep_residual_norm,
                     rows=[(row(norm_mix_post), d, 0), (row(norm_ffn_pre), d, 0)],
                     blks=[((xp, xs), d, 0)], n_p=tp // tm_b, single_buffer_w=True, name="proj_o")

    act, ffn_tail_p, ffn_tail_s = _ffn_up_conv(h2, w_up, ffn_conv_w.astype(F32), row(ffn_conv_b),
                                               _pad_hist(cache_s[3], SUBLANES), name="ffn_up", **seq_kw)
    yp, ys = _matmul(act, [(w_down.astype(BF16), d, 0)], nj=1, order="row", tm=tm_o,
                     outs=[(d, d, F32, True)], epilogue=_ep_residual,
                     rows=[(row(norm_ffn_post), d, 0)], blks=[(x1, d, 0)], n_p=tp // tm_o,
                     single_buffer_w=True, name="ffn_down")

    new_caches = []
    for si, (row0, nseq, seqlen) in enumerate(segs):
        new_caches.append((_tail_rows(u, row0, nseq, seqlen, wa - 1),
                           _tail_rows(xbc, row0, nseq, seqlen, wb - 1),
                           new_states[si],
                           _conv_tails((ffn_tail_p, ffn_tail_s)[si], wf - 1)))
    return yp, ys, new_caches


def kernel(x_prompt, x_sample, cache_conv_a, cache_conv_b, state_ssd, cache_ffn_conv, norm_mix_pre, w_in, b_gate, conv_a_w, conv_a_b, ln_a_g, ln_a_b, w_a_out, conv_b_w, conv_b_b, dt_bias, a_log, d_skip, ssd_norm_g, w_b_out, w_o, norm_mix_post, norm_ffn_pre, w_up, ffn_conv_w, ffn_conv_b, w_down, norm_ffn_post):
    weights = (norm_mix_pre, w_in, b_gate, conv_a_w, conv_a_b, ln_a_g, ln_a_b, w_a_out, conv_b_w,
               conv_b_b, dt_bias, a_log, d_skip, ssd_norm_g, w_b_out, w_o, norm_mix_post,
               norm_ffn_pre, w_up, ffn_conv_w, ffn_conv_b, w_down, norm_ffn_post)
    depth = w_in.shape[0]
    bp, lp, d = x_prompt.shape
    bs, ls, _ = x_sample.shape
    tp = bp * lp
    segs = ((0, bp, lp), (tp, bs, ls))
    xp = x_prompt.reshape(tp, d)
    xs = x_sample.reshape(bs * ls, d)
    dt_in = x_prompt.dtype
    outs_p = ([], [], [], [])
    outs_s = ([], [], [], [])
    for layer in range(depth):
        wl = tuple(w[layer] for w in weights)
        cache_s = (cache_conv_a[layer], cache_conv_b[layer], state_ssd[layer], cache_ffn_conv[layer])
        xp, xs, (new_p, new_s) = _layer(xp, xs, segs, cache_s, wl)
        for lst, v in zip(outs_p, new_p):
            lst.append(v.astype(dt_in))
        for lst, v in zip(outs_s, new_s):
            lst.append(v.astype(dt_in))
    return (xp.reshape(bp, lp, d), xs.reshape(bs, ls, d),
            jnp.stack(outs_p[0]), jnp.stack(outs_p[1]), jnp.stack(outs_p[2]), jnp.stack(outs_p[3]),
            jnp.stack(outs_s[0]), jnp.stack(outs_s[1]), jnp.stack(outs_s[2]), jnp.stack(outs_s[3]))
```
